```python
import jax
import jax.numpy as jnp
from jax import lax
import numpy as np

D_MODEL = 4096
BATCH = 8
SEQ = 2048
DEPTH = 1

CTX_LEN = 256
GRID_W = 64
F_GROUPS = 4
F_WIDTH = D_MODEL // 2
F_GROUP_DIM = F_WIDTH // F_GROUPS
NA_HEAD_DIM = 128
NA_WIDTH = D_MODEL // 2
NA_HEADS = NA_WIDTH // NA_HEAD_DIM
NA_ROWS_MAX = 8
NA_COLS = 16
NA_QBLOCK = NA_COLS
NA_REGION_W = 2 * NA_COLS
N_EXPERTS = 16
EXPERT_FF = D_MODEL // 2
EC_FACTOR = 2
N_MOD = 6
RMS_EPS = 1e-6
OFF_Q = F_WIDTH
OFF_K = OFF_Q + NA_WIDTH
OFF_V = OFF_K + NA_WIDTH
OFF_G = OFF_V + NA_WIDTH
IN_WIDTH = OFF_G + 2 * D_MODEL

kernel_name = 'hybrid_fourier_natten_ec_dit'


def rms_norm(x, g):
    xf = x.astype(jnp.float32)
    y = xf * lax.rsqrt(jnp.mean(xf * xf, axis=-1, keepdims=True) + RMS_EPS)
    return (y * g.astype(jnp.float32)).astype(x.dtype)


def ada_params(cond, w_mod, b_mod):
    m = jax.nn.silu(cond) @ w_mod + b_mod
    return jnp.split(m, N_MOD, axis=-1)


def modulate(h, shift, scale):
    return h * (1 + scale) + shift


def split_heads(t):
    b, l, _ = t.shape
    return t.reshape(b, l, NA_HEADS, NA_HEAD_DIM)


def project_in(hn, w_in, b_gate):
    z = hn @ w_in
    u_f, q, k, v, g = jnp.split(z, [OFF_Q, OFF_K, OFF_V, OFF_G], axis=-1)
    g = jax.nn.sigmoid((g + b_gate).astype(jnp.float32)).astype(hn.dtype)
    g_fourier, g_na = jnp.split(g, 2, axis=-1)
    return u_f, split_heads(q), split_heads(k), split_heads(v), g_fourier, g_na


def fourier_mix(u):
    b, l, _ = u.shape
    ug = u.reshape(b, l, F_GROUPS, F_GROUP_DIM).astype(jnp.float32)
    z = jnp.fft.fftn(ug, axes=(1, 3), norm='ortho')
    return jnp.real(z).reshape(b, l, F_WIDTH).astype(u.dtype)


def neighbourhood_attention(q, k, v, k_ctx, v_ctx, rel_bias):
    b, l, h, dh = q.shape
    rows = l // GRID_W
    kr = min(NA_ROWS_MAX, rows)
    n_cb = GRID_W // NA_QBLOCK
    scale = dh ** -0.5
    q_cols = np.arange(GRID_W).reshape(n_cb, NA_QBLOCK)
    reg_start = np.clip(q_cols[:, 0] - NA_COLS // 2, 0, GRID_W - NA_REGION_W)
    key_cols = reg_start[:, None] + np.arange(NA_REGION_W)[None, :]
    win_start = np.clip(q_cols - NA_COLS // 2, 0, GRID_W - NA_COLS)
    rel_col = key_cols[:, None, :] - win_start[:, :, None]
    col_mask = (rel_col >= 0) & (rel_col < NA_COLS)
    dc_idx = np.clip(key_cols[:, None, :] - q_cols[:, :, None] + NA_COLS - 1, 0, 2 * NA_COLS - 2)
    bias_c = rel_bias.astype(jnp.float32)[:, :, dc_idx]
    neg = jnp.finfo(jnp.float32).min

    kf = k.reshape(b, rows, GRID_W, h, dh)
    vf = v.reshape(b, rows, GRID_W, h, dh)
    q_rows = jnp.moveaxis(q.reshape(b, rows, n_cb, NA_QBLOCK, h, dh), 1, 0)
    n_loc = kr * NA_REGION_W

    def one_row(args):
        r, qr = args
        rs = jnp.clip(r - kr // 2, 0, rows - kr)
        kw = lax.dynamic_slice_in_dim(kf, rs, kr, axis=1)[:, :, key_cols]
        vw = lax.dynamic_slice_in_dim(vf, rs, kr, axis=1)[:, :, key_cols]
        s_loc = jnp.einsum('bcqhd,bicjhd->bhcqij', qr, kw).astype(jnp.float32) * scale
        dr_idx = rs + jnp.arange(kr) - r + NA_ROWS_MAX - 1
        bias = jnp.transpose(jnp.take(bias_c, dr_idx, axis=1), (0, 2, 3, 1, 4))
        s_loc = jnp.where(col_mask[:, :, None, :], s_loc + bias[None], neg)
        s_loc = s_loc.reshape(b, h, n_cb, NA_QBLOCK, n_loc)
        s_ctx = jnp.einsum('bcqhd,bjhd->bhcqj', qr, k_ctx).astype(jnp.float32) * scale
        p = jax.nn.softmax(jnp.concatenate([s_loc, s_ctx], axis=-1), axis=-1).astype(v.dtype)
        p_loc = p[..., :n_loc].reshape(b, h, n_cb, NA_QBLOCK, kr, NA_REGION_W)
        p_ctx = p[..., n_loc:]
        return (jnp.einsum('bhcqij,bicjhd->bcqhd', p_loc, vw)
                + jnp.einsum('bhcqj,bjhd->bcqhd', p_ctx, v_ctx))

    o = lax.map(one_row, (jnp.arange(rows), q_rows))
    return jnp.moveaxis(o, 0, 1).reshape(b, l, h * dh)


def context_attention(q, k, v):
    b, l, h, dh = q.shape
    s = jnp.einsum('blhd,bjhd->bhlj', q, k).astype(jnp.float32) * dh ** -0.5
    p = jax.nn.softmax(s, axis=-1).astype(v.dtype)
    return jnp.einsum('bhlj,bjhd->blhd', p, v).reshape(b, l, h * dh)


def merge_branches(u_f, o_na, g_fourier, g_na, w_fourier, w_na_out, w_out):
    y_f = fourier_mix(u_f) @ w_fourier
    y_n = o_na @ w_na_out
    return (g_fourier * y_f + g_na * y_n) @ w_out


def expert_choice_ffn(h, w_router, w1, w3, w2):
    b, n, _ = h.shape
    cap = EC_FACTOR * n // N_EXPERTS
    aff = jax.nn.softmax((h @ w_router).astype(jnp.float32), axis=-1)
    g, idx = lax.top_k(jnp.swapaxes(aff, 1, 2), cap)
    bidx = jnp.arange(b)[:, None, None]
    xin = h[bidx, idx]
    hid = jax.nn.silu(jnp.einsum('becd,edf->becf', xin, w1)) * jnp.einsum('becd,edf->becf', xin, w3)
    out = jnp.einsum('becf,efd->becd', hid, w2) * g[..., None].astype(h.dtype)
    return jnp.zeros_like(h).at[bidx, idx].add(out)


def setup_inputs(seed: int = 0) -> dict:
    key = jax.random.key(seed)
    ks = jax.random.split(key, 20)
    L = DEPTH

    def nrm(k, shape, scale):
        return jax.random.normal(k, shape, jnp.float32) * scale

    return {
        'x': nrm(ks[0], (BATCH, SEQ, D_MODEL), 1.0),
        'c': nrm(ks[1], (BATCH, D_MODEL), 1.0),
        'ctx': nrm(ks[2], (BATCH, CTX_LEN, D_MODEL), 1.0),
        'c_ctx': nrm(ks[3], (D_MODEL,), 1.0),
        'w_mod': nrm(ks[4], (L, D_MODEL, N_MOD * D_MODEL), 0.5 * D_MODEL ** -0.5),
        'b_mod': nrm(ks[5], (L, N_MOD * D_MODEL), 0.01),
        'norm_mix_g': 1.0 + nrm(ks[6], (L, D_MODEL), 0.01),
        'w_in': nrm(ks[7], (L, D_MODEL, IN_WIDTH), D_MODEL ** -0.5),
        'b_gate': nrm(ks[8], (L, 2 * D_MODEL), 0.01),
        'w_fourier': nrm(ks[9], (L, F_WIDTH, D_MODEL), F_WIDTH ** -0.5),
        'na_rel_bias': nrm(ks[10], (L, NA_HEADS, 2 * NA_ROWS_MAX - 1, 2 * NA_COLS - 1), 0.1),
        'w_na_out': nrm(ks[11], (L, NA_WIDTH, D_MODEL), NA_WIDTH ** -0.5),
        'w_out': nrm(ks[12], (L, D_MODEL, D_MODEL), D_MODEL ** -0.5),
        'norm_ffn_g': 1.0 + nrm(ks[13], (L, D_MODEL), 0.01),
        'w_router': nrm(ks[14], (L, D_MODEL, N_EXPERTS), D_MODEL ** -0.5),
        'w1': nrm(ks[15], (L, N_EXPERTS, D_MODEL, EXPERT_FF), D_MODEL ** -0.5),
        'w3': nrm(ks[16], (L, N_EXPERTS, D_MODEL, EXPERT_FF), D_MODEL ** -0.5),
        'w2': nrm(ks[17], (L, N_EXPERTS, EXPERT_FF, D_MODEL), EXPERT_FF ** -0.5),
        'final_norm_g': 1.0 + nrm(ks[18], (D_MODEL,), 0.01),
    }


def reference(x, c, ctx, c_ctx, w_mod, b_mod, norm_mix_g, w_in, b_gate, w_fourier, na_rel_bias,
              w_na_out, w_out, norm_ffn_g, w_router, w1, w3, w2, final_norm_g):
    for layer in range(DEPTH):
        last = layer == DEPTH - 1
        sh_m, sc_m, gt_m, sh_f, sc_f, gt_f = [t[:, None, :] for t in ada_params(c, w_mod[layer], b_mod[layer])]
        csh_m, csc_m, cgt_m, csh_f, csc_f, cgt_f = ada_params(c_ctx, w_mod[layer], b_mod[layer])

        xn = modulate(rms_norm(x, norm_mix_g[layer]), sh_m, sc_m)
        cn = modulate(rms_norm(ctx, norm_mix_g[layer]), csh_m, csc_m)
        u_f, q, k, v, g_fr, g_na = project_in(xn, w_in[layer], b_gate[layer])
        if last:
            k_c, v_c = [split_heads(t) for t in jnp.split(cn @ w_in[layer][:, OFF_K:OFF_G], 2, axis=-1)]
        else:
            u_fc, q_c, k_c, v_c, g_frc, g_nac = project_in(cn, w_in[layer], b_gate[layer])
        o_na = neighbourhood_attention(q, k, v, k_c, v_c, na_rel_bias[layer])
        x = x + gt_m * merge_branches(u_f, o_na, g_fr, g_na, w_fourier[layer], w_na_out[layer], w_out[layer])

        xn = modulate(rms_norm(x, norm_ffn_g[layer]), sh_f, sc_f)
        x = x + gt_f * expert_choice_ffn(xn, w_router[layer], w1[layer], w3[layer], w2[layer])

        if not last:
            o_c = context_attention(q_c, k_c, v_c)
            ctx = ctx + cgt_m * merge_branches(u_fc, o_c, g_frc, g_nac, w_fourier[layer], w_na_out[layer], w_out[layer])
            cn = modulate(rms_norm(ctx, norm_ffn_g[layer]), csh_f, csc_f)
            ctx = ctx + cgt_f * expert_choice_ffn(cn, w_router[layer], w1[layer], w3[layer], w2[layer])
    return rms_norm(x, final_norm_g)
```

```python
import functools

import numpy as np
import jax
import jax.numpy as jnp
from jax import lax
from jax.experimental import pallas as pl
from jax.experimental.pallas import tpu as pltpu

GRID_W = 64
F_GROUPS = 4
NA_HEAD_DIM = 128
NA_ROWS_MAX = 8
NA_COLS = 16
EC_FACTOR = 2
N_MOD = 6
RMS_EPS = 1e-6

LANES = 128
VMEM_LIMIT_BYTES = 56 * 1024 * 1024
BISECT_STEPS = 160

F32 = jnp.float32
BF16 = jnp.bfloat16
NT_DIMS = (((1,), (1,)), ((), ()))


def _params(n_axes):
    return pltpu.CompilerParams(
        dimension_semantics=("arbitrary",) * n_axes, vmem_limit_bytes=VMEM_LIMIT_BYTES)


def _tile(dim, pref):
    t = min(dim, pref)
    while dim % t:
        t -= 1
    return t


def _dot(a, b):
    return jnp.dot(a, b, preferred_element_type=F32)


def _rms_modulate(x, gain, scale, shift):
    ms = jnp.mean(x * x, axis=-1, keepdims=True)
    y = x * lax.rsqrt(ms + RMS_EPS) * gain
    return y * (1.0 + scale) + shift


def _ada_kernel(c_ref, w_ref, b_ref, o_ref):
    cond = c_ref[...]
    act = cond * jax.nn.sigmoid(cond)
    o_ref[...] = _dot(act.astype(BF16), w_ref[...].astype(BF16)) + b_ref[...]


def _ada_params(cond, w_mod, b_mod):
    m, d = cond.shape
    n = w_mod.shape[1]
    tn = _tile(n, 512)
    return pl.pallas_call(
        _ada_kernel,
        grid=(n // tn,),
        in_specs=[pl.BlockSpec((m, d), lambda j: (0, 0)),
                  pl.BlockSpec((d, tn), lambda j: (0, j)),
                  pl.BlockSpec((1, tn), lambda j: (0, j))],
        out_specs=pl.BlockSpec((m, tn), lambda j: (0, j)),
        out_shape=jax.ShapeDtypeStruct((m, n), F32),
        compiler_params=_params(1),
    )(cond, w_mod, b_mod.reshape(1, n))


def _norm_mod_kernel(x_ref, g_ref, sc_ref, sh_ref, o_ref):
    o_ref[0] = _rms_modulate(x_ref[0], g_ref[...], sc_ref[0], sh_ref[0]).astype(o_ref.dtype)


def _norm_mod(x, gain, mod, k_shift, k_scale):
    b, l, d = x.shape
    per_batch = mod.shape[0] == b
    tm = _tile(l, 512)
    bsel = (lambda i: i) if per_batch else (lambda i: 0)
    return pl.pallas_call(
        _norm_mod_kernel,
        grid=(b, l // tm),
        in_specs=[pl.BlockSpec((1, tm, d), lambda bi, i: (bi, i, 0)),
                  pl.BlockSpec((1, d), lambda bi, i: (0, 0)),
                  pl.BlockSpec((1, 1, d), lambda bi, i: (bsel(bi), 0, k_scale)),
                  pl.BlockSpec((1, 1, d), lambda bi, i: (bsel(bi), 0, k_shift))],
        out_specs=pl.BlockSpec((1, tm, d), lambda bi, i: (bi, i, 0)),
        out_shape=jax.ShapeDtypeStruct((b, l, d), BF16),
        compiler_params=_params(2),
    )(x, gain.reshape(1, d), mod, mod)


def _proj_in_kernel(a_ref, w_ref, b_ref, o_ref, *, gate_block0):
    acc = _dot(a_ref[...], w_ref[...])
    j = pl.program_id(1)

    @pl.when(j < gate_block0)
    def _():
        o_ref[...] = acc.astype(o_ref.dtype)

    @pl.when(j >= gate_block0)
    def _():
        o_ref[...] = jax.nn.sigmoid(acc + b_ref[...]).astype(o_ref.dtype)


def _proj_in(a, w, b_full, off_gate):
    m, k = a.shape
    n = w.shape[1]
    tm = _tile(m, 1024)
    tn = _tile(np.gcd(n, off_gate), 512)
    return pl.pallas_call(
        functools.partial(_proj_in_kernel, gate_block0=off_gate // tn),
        grid=(m // tm, n // tn),
        in_specs=[pl.BlockSpec((tm, k), lambda i, j: (i, 0)),
                  pl.BlockSpec((k, tn), lambda i, j: (0, j)),
                  pl.BlockSpec((1, tn), lambda i, j: (0, j))],
        out_specs=pl.BlockSpec((tm, tn), lambda i, j: (i, j)),
        out_shape=jax.ShapeDtypeStruct((m, n), BF16),
        compiler_params=_params(2),
    )(a, w, b_full)


def _mm_kernel(a_ref, w_ref, o_ref):
    o_ref[...] = _dot(a_ref[...], w_ref[...]).astype(o_ref.dtype)


def _matmul_cols(a, w, col0, ncols):
    m, k = a.shape
    tm = _tile(m, 1024)
    tn = _tile(np.gcd(ncols, col0) if col0 else ncols, 512)
    j0 = col0 // tn
    return pl.pallas_call(
        _mm_kernel,
        grid=(m // tm, ncols // tn),
        in_specs=[pl.BlockSpec((tm, k), lambda i, j: (i, 0)),
                  pl.BlockSpec((k, tn), lambda i, j: (0, j0 + j))],
        out_specs=pl.BlockSpec((tm, tn), lambda i, j: (i, j)),
        out_shape=jax.ShapeDtypeStruct((m, ncols), BF16),
        compiler_params=_params(2),
    )(a, w)


def _dft_tables(n, scale):
    k = np.arange(n)
    ang = 2.0 * np.pi * ((k[:, None] * k[None, :]) % n) / n
    return np.cos(ang) * scale, np.sin(ang) * scale


def _fourier_chan_kernel(u_ref, t_ref, o_ref):
    dg = u_ref.shape[-1]
    res = _dot(u_ref[0], t_ref[...])
    o_ref[0, 0] = res[:, :dg].astype(o_ref.dtype)
    o_ref[0, 1] = res[:, dg:].astype(o_ref.dtype)


def _fourier_chan(z3, f_width):
    b, l, _ = z3.shape
    dg = f_width // F_GROUPS
    cd, sd = _dft_tables(dg, dg ** -0.5)
    table = jnp.asarray(np.concatenate([cd, sd], axis=1), dtype=BF16)
    tm = _tile(l, 1024)
    return pl.pallas_call(
        _fourier_chan_kernel,
        grid=(b, l // tm, F_GROUPS),
        in_specs=[pl.BlockSpec((1, tm, dg), lambda bi, i, g: (bi, i, g)),
                  pl.BlockSpec((dg, 2 * dg), lambda bi, i, g: (0, 0))],
        out_specs=pl.BlockSpec((1, 2, tm, dg), lambda bi, i, g: (bi, 0, i, g)),
        out_shape=jax.ShapeDtypeStruct((b, 2, l, f_width), BF16),
        compiler_params=_params(3),
    )(z3, table)


def _fourier_pos_kernel(t_ref, ab_ref, o_ref):
    o_ref[0] = _dot(t_ref[...], ab_ref[0]).astype(o_ref.dtype)


def _fourier_pos(ab):
    b, l2, fw = ab.shape
    l = l2 // 2
    cl, sl = _dft_tables(l, l ** -0.5)
    table = jnp.asarray(np.concatenate([cl, -sl], axis=1), dtype=BF16)
    tm = _tile(l, 512)
    tn = _tile(fw, 512)
    return pl.pallas_call(
        _fourier_pos_kernel,
        grid=(b, fw // tn, l // tm),
        in_specs=[pl.BlockSpec((tm, l2), lambda bi, j, i: (i, 0)),
                  pl.BlockSpec((1, l2, tn), lambda bi, j, i: (bi, 0, j))],
        out_specs=pl.BlockSpec((1, tm, tn), lambda bi, j, i: (bi, i, j)),
        out_shape=jax.ShapeDtypeStruct((b, l, fw), BF16),
        compiler_params=_params(3),
    )(table, ab)


def _na_geometry(rows):
    kr = min(NA_ROWS_MAX, rows)
    starts = [int(np.clip(r - kr // 2, 0, rows - kr)) - r + NA_ROWS_MAX - 1 for r in range(rows)]
    return kr, min(starts), max(starts) - min(starts) + 1


def _na_bias_slabs(rel_bias, rows):
    kr, min_start, n_slabs = _na_geometry(rows)
    cols = np.arange(GRID_W)
    win_start = np.clip(cols - NA_COLS // 2, 0, GRID_W - NA_COLS)
    rel_col = cols[None, :] - win_start[:, None]
    col_mask = (rel_col >= 0) & (rel_col < NA_COLS)
    dc_idx = np.clip(cols[None, :] - cols[:, None] + NA_COLS - 1, 0, 2 * NA_COLS - 2)
    neg = jnp.finfo(F32).min
    bias_c = jnp.where(col_mask[None, None], rel_bias.astype(F32)[:, :, dc_idx], neg)
    slabs = jnp.stack([bias_c[:, min_start + s:min_start + s + kr] for s in range(n_slabs)], axis=1)
    slabs = jnp.transpose(slabs, (0, 1, 3, 2, 4))
    h = rel_bias.shape[0]
    return slabs.reshape(h, n_slabs, GRID_W, kr * GRID_W)


def _na_kernel(q_ref, k_ref, v_ref, kc_ref, vc_ref, bias_ref, o_ref, *, rows, kr, min_start, scale):
    kc = kc_ref[0]
    vc = vc_ref[0]

    def one_row(r, carry):
        rs = jnp.clip(r - kr // 2, 0, rows - kr)
        slab = rs - r + (NA_ROWS_MAX - 1 - min_start)
        q0 = pl.multiple_of(r * GRID_W, GRID_W)
        k0 = pl.multiple_of(rs * GRID_W, GRID_W)
        qr = q_ref[0, pl.ds(q0, GRID_W), :]
        kw = k_ref[0, pl.ds(k0, kr * GRID_W), :]
        vw = v_ref[0, pl.ds(k0, kr * GRID_W), :]
        s_loc = lax.dot_general(qr, kw, NT_DIMS, preferred_element_type=F32) * scale + bias_ref[0, slab]
        s_ctx = lax.dot_general(qr, kc, NT_DIMS, preferred_element_type=F32) * scale
        m = jnp.maximum(jnp.max(s_loc, axis=-1, keepdims=True), jnp.max(s_ctx, axis=-1, keepdims=True))
        p_loc = jnp.exp(s_loc - m)
        p_ctx = jnp.exp(s_ctx - m)
        inv = 1.0 / (jnp.sum(p_loc, axis=-1, keepdims=True) + jnp.sum(p_ctx, axis=-1, keepdims=True))
        o = _dot((p_loc * inv).astype(BF16), vw) + _dot((p_ctx * inv).astype(BF16), vc)
        o_ref[0, pl.ds(q0, GRID_W), :] = o.astype(o_ref.dtype)
        return carry

    lax.fori_loop(0, rows, one_row, 0)


def _neighbourhood_attention(z3, zc3, bias_slabs, off_q, na_width):
    b, l, _ = z3.shape
    ctx_len = zc3.shape[1]
    dh = NA_HEAD_DIM
    heads = na_width // dh
    rows = l // GRID_W
    kr, min_start, n_slabs = _na_geometry(rows)
    qb, kb, vb = off_q // dh, (off_q + na_width) // dh, (off_q + 2 * na_width) // dh
    kern = functools.partial(_na_kernel, rows=rows, kr=kr, min_start=min_start, scale=dh ** -0.5)
    return pl.pallas_call(
        kern,
        grid=(heads, b),
        in_specs=[pl.BlockSpec((1, l, dh), lambda h, bi: (bi, 0, qb + h)),
                  pl.BlockSpec((1, l, dh), lambda h, bi: (bi, 0, kb + h)),
                  pl.BlockSpec((1, l, dh), lambda h, bi: (bi, 0, vb + h)),
                  pl.BlockSpec((1, ctx_len, dh), lambda h, bi: (bi, 0, h)),
                  pl.BlockSpec((1, ctx_len, dh), lambda h, bi: (bi, 0, heads + h)),
                  pl.BlockSpec((1, n_slabs, GRID_W, kr * GRID_W), lambda h, bi: (h, 0, 0, 0))],
        out_specs=pl.BlockSpec((1, l, dh), lambda h, bi: (bi, 0, h)),
        out_shape=jax.ShapeDtypeStruct((b, l, na_width), BF16),
        compiler_params=_params(2),
    )(z3, z3, z3, zc3, zc3, bias_slabs)


def _merge_kernel(yf_ref, on_ref, wf_ref, wn_ref, gf_ref, gn_ref, o_ref):
    y_f = _dot(yf_ref[...], wf_ref[...])
    y_n = _dot(on_ref[...], wn_ref[...])
    o_ref[...] = (gf_ref[...].astype(F32) * y_f + gn_ref[...].astype(F32) * y_n).astype(o_ref.dtype)


def _merge(y_four, o_na, w_f, w_n, z, off_gate):
    m, kf = y_four.shape
    kn = o_na.shape[1]
    d = w_f.shape[1]
    tm = _tile(m, 1024)
    tn = _tile(np.gcd(d, off_gate), 512)
    gf0, gn0 = off_gate // tn, (off_gate + d) // tn
    return pl.pallas_call(
        _merge_kernel,
        grid=(m // tm, d // tn),
        in_specs=[pl.BlockSpec((tm, kf), lambda i, j: (i, 0)),
                  pl.BlockSpec((tm, kn), lambda i, j: (i, 0)),
                  pl.BlockSpec((kf, tn), lambda i, j: (0, j)),
                  pl.BlockSpec((kn, tn), lambda i, j: (0, j)),
                  pl.BlockSpec((tm, tn), lambda i, j: (i, gf0 + j)),
                  pl.BlockSpec((tm, tn), lambda i, j: (i, gn0 + j))],
        out_specs=pl.BlockSpec((tm, tn), lambda i, j: (i, j)),
        out_shape=jax.ShapeDtypeStruct((m, d), BF16),
        compiler_params=_params(2),
    )(y_four, o_na, w_f, w_n, z, z)


def _out_proj_kernel(a_ref, w_ref, x_ref, gt_ref, o_ref):
    o_ref[...] = x_ref[...] + gt_ref[0] * _dot(a_ref[...], w_ref[...])


def _out_proj_residual(a, w, x2d, mod, k_gate, seq):
    m, k = a.shape
    d = w.shape[1]
    tm = _tile(seq, 1024)
    tn = _tile(d, 512)
    gate0 = k_gate * (d // tn)
    return pl.pallas_call(
        _out_proj_kernel,
        grid=(m // tm, d // tn),
        in_specs=[pl.BlockSpec((tm, k), lambda i, j: (i, 0)),
                  pl.BlockSpec((k, tn), lambda i, j: (0, j)),
                  pl.BlockSpec((tm, tn), lambda i, j: (i, j)),
                  pl.BlockSpec((1, 1, tn), lambda i, j: (i * tm // seq, 0, gate0 + j))],
        out_specs=pl.BlockSpec((tm, tn), lambda i, j: (i, j)),
        out_shape=jax.ShapeDtypeStruct((m, d), F32),
        compiler_params=_params(2),
    )(a, w, x2d, mod)


def _router_kernel(x_ref, g_ref, sc_ref, sh_ref, wr_ref, aff_ref):
    xn = _rms_modulate(x_ref[0], g_ref[...], sc_ref[0], sh_ref[0])
    x_hi = xn.astype(BF16)
    x_lo = (xn - x_hi.astype(F32)).astype(BF16)
    w = wr_ref[...]
    w_hi = w.astype(BF16)
    w_lo = (w - w_hi.astype(F32)).astype(BF16)
    logits = (lax.dot_general(w_hi, x_hi, NT_DIMS, preferred_element_type=F32)
              + lax.dot_general(w_lo, x_hi, NT_DIMS, preferred_element_type=F32)
              + lax.dot_general(w_hi, x_lo, NT_DIMS, preferred_element_type=F32))
    ex = jnp.exp(logits - jnp.max(logits, axis=0, keepdims=True))
    aff_ref[0] = ex / jnp.sum(ex, axis=0, keepdims=True)


def _router(x, gain, mod, k_shift, k_scale, w_router_t):
    b, l, d = x.shape
    e = w_router_t.shape[0]
    tm = _tile(l, 512)
    return pl.pallas_call(
        _router_kernel,
        grid=(b, l // tm),
        in_specs=[pl.BlockSpec((1, tm, d), lambda bi, i: (bi, i, 0)),
                  pl.BlockSpec((1, d), lambda bi, i: (0, 0)),
                  pl.BlockSpec((1, 1, d), lambda bi, i: (bi, 0, k_scale)),
                  pl.BlockSpec((1, 1, d), lambda bi, i: (bi, 0, k_shift)),
                  pl.BlockSpec((e, d), lambda bi, i: (0, 0))],
        out_specs=pl.BlockSpec((1, e, tm), lambda bi, i: (bi, 0, i)),
        out_shape=jax.ShapeDtypeStruct((b, e, l), F32),
        compiler_params=_params(2),
    )(x, gain.reshape(1, d), mod, mod, w_router_t)


def _select_kernel(a_ref, tri_ref, idx_ref, pos_ref, *, cap):
    a = a_ref[...]
    n_rows, l = a.shape
    capf = float(cap)

    def count(mask):
        return jnp.sum(jnp.where(mask, 1.0, 0.0), axis=-1, keepdims=True)

    def halve(_, lo_hi):
        lo, hi = lo_hi
        mid = 0.5 * lo + 0.5 * hi
        ok = count(a_ref[...] >= mid) >= capf
        return jnp.where(ok, mid, lo), jnp.where(ok, hi, mid)

    _, hi = lax.fori_loop(0, BISECT_STEPS, halve,
                          (jnp.zeros((n_rows, 1), F32), jnp.full((n_rows, 1), 2.0, F32)))
    tau = jnp.max(jnp.where(a < hi, a, -1.0), axis=-1, keepdims=True)
    above = a > tau
    tied = a == tau
    need = capf - count(above)
    tri = tri_ref[...]
    tied_before = _dot(jnp.where(tied, 1.0, 0.0).astype(BF16), tri)
    chosen = above | (tied & (tied_before < need))
    pos = _dot(jnp.where(chosen, 1.0, 0.0).astype(BF16), tri)
    pos_ref[...] = jnp.where(chosen, pos, -1.0)

    tok = lax.broadcasted_iota(jnp.int32, (cap, l), 1).astype(F32)
    slot = lax.broadcasted_iota(jnp.int32, (cap, l), 0).astype(F32)
    lane = lax.broadcasted_iota(jnp.int32, (cap, n_rows), 1)

    def compact(r, acc):
        hit = pos_ref[pl.ds(r, 1), :] == slot
        ids = jnp.sum(jnp.where(hit, tok, 0.0), axis=-1, keepdims=True)
        return jnp.where(lane == r, ids, acc)

    idx_ref[...] = lax.fori_loop(0, n_rows, compact, jnp.zeros((cap, n_rows), F32)).astype(jnp.int32)


def _select(aff_rows, cap):
    n_rows, l = aff_rows.shape
    tri = jnp.asarray(np.triu(np.ones((l, l), np.float32), k=1), dtype=BF16)
    return pl.pallas_call(
        functools.partial(_select_kernel, cap=cap),
        grid=(1,),
        in_specs=[pl.BlockSpec((n_rows, l), lambda i: (0, 0)),
                  pl.BlockSpec((l, l), lambda i: (0, 0))],
        out_specs=pl.BlockSpec((cap, n_rows), lambda i: (0, 0)),
        out_shape=jax.ShapeDtypeStruct((cap, n_rows), jnp.int32),
        scratch_shapes=[pltpu.VMEM((n_rows, l), F32)],
        compiler_params=_params(1),
    )(aff_rows, tri)


def _gather_copy(x_hbm, buf, sem, row, j):
    return pltpu.make_async_copy(x_hbm.at[pl.ds(row, 1)], buf.at[pl.ds(j, 1)], sem)


def _gather_norm_kernel(rows_ref, x_hbm, g_ref, sc_ref, sh_ref, o_ref, buf, sem, *, n_rows):
    base = pl.program_id(0) * n_rows

    def start(j, carry):
        _gather_copy(x_hbm, buf, sem, rows_ref[base + j], j).start()
        return carry

    lax.fori_loop(0, n_rows, start, 0)

    def wait(j, carry):
        _gather_copy(x_hbm, buf, sem, 0, j).wait()
        return carry

    lax.fori_loop(0, n_rows, wait, 0)
    o_ref[...] = _rms_modulate(buf[...], g_ref[...], sc_ref[0], sh_ref[0]).astype(o_ref.dtype)


def _gather_norm(flat_rows, x2d, gain, mod, k_shift, k_scale, cap, n_batch):
    n = flat_rows.shape[0]
    d = x2d.shape[1]
    r = _tile(cap, 256)
    per_b = cap // r
    bsel = lambda i, rows: (i // per_b) % n_batch
    grid_spec = pltpu.PrefetchScalarGridSpec(
        num_scalar_prefetch=1,
        grid=(n // r,),
        in_specs=[pl.BlockSpec(memory_space=pl.ANY),
                  pl.BlockSpec((1, d), lambda i, rows: (0, 0)),
                  pl.BlockSpec((1, 1, d), lambda i, rows: (bsel(i, rows), 0, k_scale)),
                  pl.BlockSpec((1, 1, d), lambda i, rows: (bsel(i, rows), 0, k_shift))],
        out_specs=pl.BlockSpec((r, d), lambda i, rows: (i, 0)),
        scratch_shapes=[pltpu.VMEM((r, d), F32), pltpu.SemaphoreType.DMA],
    )
    return pl.pallas_call(
        functools.partial(_gather_norm_kernel, n_rows=r),
        grid_spec=grid_spec,
        out_shape=jax.ShapeDtypeStruct((n, d), BF16),
        compiler_params=_params(1),
    )(flat_rows, x2d, gain.reshape(1, d), mod, mod)


def _ffn_up_kernel(x_ref, w1_ref, w3_ref, o_ref, w1_s, w3_s):
    @pl.when(pl.program_id(2) == 0)
    def _():
        w1_s[...] = w1_ref[0].astype(BF16)
        w3_s[...] = w3_ref[0].astype(BF16)

    xin = x_ref[...]
    a = _dot(xin, w1_s[...])
    o_ref[...] = (a * jax.nn.sigmoid(a) * _dot(xin, w3_s[...])).astype(o_ref.dtype)


def _ffn_up(xin, w1, w3):
    n, d = xin.shape
    e, _, ff = w1.shape
    rows_e = n // e
    tm = _tile(rows_e, 1024)
    tn = _tile(ff, 256)
    per_e = rows_e // tm
    return pl.pallas_call(
        _ffn_up_kernel,
        grid=(e, ff // tn, per_e),
        in_specs=[pl.BlockSpec((tm, d), lambda ei, j, i: (ei * per_e + i, 0)),
                  pl.BlockSpec((1, d, tn), lambda ei, j, i: (ei, 0, j)),
                  pl.BlockSpec((1, d, tn), lambda ei, j, i: (ei, 0, j))],
        out_specs=pl.BlockSpec((tm, tn), lambda ei, j, i: (ei * per_e + i, j)),
        out_shape=jax.ShapeDtypeStruct((n, ff), BF16),
        scratch_shapes=[pltpu.VMEM((d, tn), BF16), pltpu.VMEM((d, tn), BF16)],
        compiler_params=_params(3),
    )(xin, w1, w3)


def _ffn_down_kernel(h_ref, w2_ref, o_ref):
    o_ref[...] = _dot(h_ref[...], w2_ref[0].astype(BF16)).astype(o_ref.dtype)


def _ffn_down(hid, w2):
    n, ff = hid.shape
    e, _, d = w2.shape
    rows_e = n // e
    tn = _tile(d, 512)
    return pl.pallas_call(
        _ffn_down_kernel,
        grid=(e, d // tn),
        in_specs=[pl.BlockSpec((rows_e, ff), lambda ei, j: (ei, 0)),
                  pl.BlockSpec((1, ff, tn), lambda ei, j: (ei, 0, j))],
        out_specs=pl.BlockSpec((rows_e, tn), lambda ei, j: (ei, j)),
        out_shape=jax.ShapeDtypeStruct((n, d), BF16),
        compiler_params=_params(2),
    )(hid, w2)


def _combine_kernel(idx_ref, out_ref, aff_ref, x_ref, gt_ref, fg_ref, o_ref, acc, *, n_experts):
    i, e = pl.program_id(1), pl.program_id(2)
    tm = acc.shape[0]
    cap = idx_ref.shape[-1]

    @pl.when(e == 0)
    def _():
        acc[...] = jnp.zeros_like(acc)

    tok = lax.broadcasted_iota(jnp.int32, (tm, cap), 0) + i * tm
    onehot = jnp.where(tok == idx_ref[0, 0], 1.0, 0.0).astype(BF16)
    aff = aff_ref[0]
    lane = lax.broadcasted_iota(jnp.int32, aff.shape, 1)
    gate = jnp.sum(jnp.where(lane == e, aff, 0.0), axis=-1, keepdims=True)
    acc[...] += gate * _dot(onehot, out_ref[0, 0])

    @pl.when(e == n_experts - 1)
    def _():
        x2 = x_ref[0] + gt_ref[0] * acc[...]
        ms = jnp.mean(x2 * x2, axis=-1, keepdims=True)
        o_ref[0] = x2 * lax.rsqrt(ms + RMS_EPS) * fg_ref[...]


def _combine(idx, out, aff, x, mod, k_gate, final_g):
    b, l, d = x.shape
    e, cap = idx.shape[1], idx.shape[3]
    tm = _tile(l, 256)
    return pl.pallas_call(
        functools.partial(_combine_kernel, n_experts=e),
        grid=(b, l // tm, e),
        in_specs=[pl.BlockSpec((1, 1, 1, cap), lambda bi, i, ei: (bi, ei, 0, 0)),
                  pl.BlockSpec((1, 1, cap, d), lambda bi, i, ei: (ei, bi, 0, 0)),
                  pl.BlockSpec((1, tm, e), lambda bi, i, ei: (bi, i, 0)),
                  pl.BlockSpec((1, tm, d), lambda bi, i, ei: (bi, i, 0)),
                  pl.BlockSpec((1, 1, d), lambda bi, i, ei: (bi, 0, k_gate)),
                  pl.BlockSpec((1, d), lambda bi, i, ei: (0, 0))],
        out_specs=pl.BlockSpec((1, tm, d), lambda bi, i, ei: (bi, i, 0)),
        out_shape=jax.ShapeDtypeStruct((b, l, d), F32),
        scratch_shapes=[pltpu.VMEM((tm, d), F32)],
        compiler_params=_params(3),
    )(idx, out, aff, x, mod, final_g.reshape(1, d))


def kernel(x, c, ctx, c_ctx, w_mod, b_mod, norm_mix_g, w_in, b_gate, w_fourier, na_rel_bias,
           w_na_out, w_out, norm_ffn_g, w_router, w1, w3, w2, final_norm_g):
    assert w_mod.shape[0] == 1, "single-layer stack only"
    b, l, d = x.shape
    ctx_len = ctx.shape[1]
    f_width = w_fourier.shape[1]
    na_width = w_na_out.shape[1]
    n_experts = w_router.shape[-1]
    in_width = w_in.shape[-1]
    off_q = f_width
    off_k = off_q + na_width
    off_g = off_k + 2 * na_width
    cap = EC_FACTOR * l // n_experts
    SH_M, SC_M, GT_M, SH_F, SC_F, GT_F = range(N_MOD)

    pad = (-(b + 1)) % 16
    cond = jnp.concatenate([c, c_ctx[None], jnp.zeros((pad, d), F32)], axis=0)
    mod_all = _ada_params(cond, w_mod[0], b_mod[0])
    mod = mod_all[:b].reshape(b, 1, N_MOD * d)
    mod_ctx = mod_all[b:b + 1].reshape(1, 1, N_MOD * d)

    w_in_bf = w_in[0].astype(BF16)
    xn = _norm_mod(x, norm_mix_g[0], mod, SH_M, SC_M)
    cn = _norm_mod(ctx, norm_mix_g[0], mod_ctx, SH_M, SC_M)
    b_full = jnp.concatenate([jnp.zeros((off_g,), F32), b_gate[0]]).reshape(1, in_width)
    z = _proj_in(xn.reshape(b * l, d), w_in_bf, b_full, off_g)
    zc = _matmul_cols(cn.reshape(b * ctx_len, d), w_in_bf, off_k, 2 * na_width)
    z3 = z.reshape(b, l, in_width)

    ab = _fourier_chan(z3, f_width)
    y_four = _fourier_pos(ab.reshape(b, 2 * l, f_width))
    bias_slabs = _na_bias_slabs(na_rel_bias[0], l // GRID_W)
    o_na = _neighbourhood_attention(z3, zc.reshape(b, ctx_len, 2 * na_width), bias_slabs, off_q, na_width)
    mixed = _merge(y_four.reshape(b * l, f_width), o_na.reshape(b * l, na_width),
                   w_fourier[0].astype(BF16), w_na_out[0].astype(BF16), z, off_g)
    x1 = _out_proj_residual(mixed, w_out[0].astype(BF16), x.reshape(b * l, d), mod, GT_M, l)

    x1_3 = x1.reshape(b, l, d)
    aff_t = _router(x1_3, norm_ffn_g[0], mod, SH_F, SC_F, w_router[0].T)
    idx_t = _select(aff_t.reshape(b * n_experts, l), cap)
    idx = idx_t.T.reshape(b, n_experts, cap)
    flat_rows = (jnp.transpose(idx, (1, 0, 2)) + (jnp.arange(b, dtype=jnp.int32) * l)[None, :, None]).reshape(-1)
    xin = _gather_norm(flat_rows, x1, norm_ffn_g[0], mod, SH_F, SC_F, cap, b)
    hid = _ffn_up(xin, w1[0], w3[0])
    out = _ffn_down(hid, w2[0])
    return _combine(idx.reshape(b, n_experts, 1, cap), out.reshape(n_experts, b, cap, d),
                    jnp.transpose(aff_t, (0, 2, 1)), x1_3, mod, GT_F, final_norm_g)
```

```python
import functools

import numpy as np
import jax
import jax.numpy as jnp
from jax import lax
from jax.experimental import pallas as pl
from jax.experimental.pallas import tpu as pltpu

GRID_W = 64
F_GROUPS = 4
NA_HEAD_DIM = 128
NA_ROWS_MAX = 8
NA_COLS = 16
EC_FACTOR = 2
N_MOD = 6
RMS_EPS = 1e-6

LANES = 128
VMEM_LIMIT_BYTES = 56 * 1024 * 1024
BISECT_STEPS = 160
NA_ROWS_PER_STEP = 8

F32 = jnp.float32
BF16 = jnp.bfloat16
NT_DIMS = (((1,), (1,)), ((), ()))


def _params(n_axes):
    return pltpu.CompilerParams(
        dimension_semantics=("arbitrary",) * n_axes, vmem_limit_bytes=VMEM_LIMIT_BYTES)


def _tile(dim, pref):
    t = min(dim, pref)
    while dim % t:
        t -= 1
    return t


def _dot(a, b):
    return jnp.dot(a, b, preferred_element_type=F32)


def _rms_modulate(x, gain, scale, shift):
    ms = jnp.mean(x * x, axis=-1, keepdims=True)
    y = x * lax.rsqrt(ms + RMS_EPS) * gain
    return y * (1.0 + scale) + shift


def _ada_kernel(c_ref, w_ref, b_ref, o_ref):
    cond = c_ref[...]
    act = cond * jax.nn.sigmoid(cond)
    o_ref[...] = _dot(act.astype(BF16), w_ref[...].astype(BF16)) + b_ref[...]


def _ada_params(cond, w_mod, b_mod):
    m, d = cond.shape
    n = w_mod.shape[1]
    tn = _tile(n, 512)
    return pl.pallas_call(
        _ada_kernel,
        grid=(n // tn,),
        in_specs=[pl.BlockSpec((m, d), lambda j: (0, 0)),
                  pl.BlockSpec((d, tn), lambda j: (0, j)),
                  pl.BlockSpec((1, tn), lambda j: (0, j))],
        out_specs=pl.BlockSpec((m, tn), lambda j: (0, j)),
        out_shape=jax.ShapeDtypeStruct((m, n), F32),
        compiler_params=_params(1),
    )(cond, w_mod, b_mod.reshape(1, n))


def _norm_mod_kernel(x_ref, g_ref, sc_ref, sh_ref, o_ref):
    o_ref[0] = _rms_modulate(x_ref[0], g_ref[...], sc_ref[0], sh_ref[0]).astype(o_ref.dtype)


def _norm_mod(x, gain, mod, k_shift, k_scale):
    b, l, d = x.shape
    per_batch = mod.shape[0] == b
    tm = _tile(l, 512)
    bsel = (lambda i: i) if per_batch else (lambda i: 0)
    return pl.pallas_call(
        _norm_mod_kernel,
        grid=(b, l // tm),
        in_specs=[pl.BlockSpec((1, tm, d), lambda bi, i: (bi, i, 0)),
                  pl.BlockSpec((1, d), lambda bi, i: (0, 0)),
                  pl.BlockSpec((1, 1, d), lambda bi, i: (bsel(bi), 0, k_scale)),
                  pl.BlockSpec((1, 1, d), lambda bi, i: (bsel(bi), 0, k_shift))],
        out_specs=pl.BlockSpec((1, tm, d), lambda bi, i: (bi, i, 0)),
        out_shape=jax.ShapeDtypeStruct((b, l, d), BF16),
        compiler_params=_params(2),
    )(x, gain.reshape(1, d), mod, mod)


def _proj_in_kernel(a_ref, w_ref, b_ref, o_ref, *, gate_block0):
    acc = _dot(a_ref[...], w_ref[...])
    j = pl.program_id(1)

    @pl.when(j < gate_block0)
    def _():
        o_ref[...] = acc.astype(o_ref.dtype)

    @pl.when(j >= gate_block0)
    def _():
        o_ref[...] = jax.nn.sigmoid(acc + b_ref[...]).astype(o_ref.dtype)


def _proj_in(a, w, b_full, off_gate):
    m, k = a.shape
    n = w.shape[1]
    tm = _tile(m, 1024)
    tn = _tile(np.gcd(n, off_gate), 1024)
    return pl.pallas_call(
        functools.partial(_proj_in_kernel, gate_block0=off_gate // tn),
        grid=(m // tm, n // tn),
        in_specs=[pl.BlockSpec((tm, k), lambda i, j: (i, 0)),
                  pl.BlockSpec((k, tn), lambda i, j: (0, j)),
                  pl.BlockSpec((1, tn), lambda i, j: (0, j))],
        out_specs=pl.BlockSpec((tm, tn), lambda i, j: (i, j)),
        out_shape=jax.ShapeDtypeStruct((m, n), BF16),
        compiler_params=_params(2),
    )(a, w, b_full)


def _mm_kernel(a_ref, w_ref, o_ref):
    o_ref[...] = _dot(a_ref[...], w_ref[...]).astype(o_ref.dtype)


def _matmul_cols(a, w, col0, ncols):
    m, k = a.shape
    tm = _tile(m, 1024)
    tn = _tile(np.gcd(ncols, col0) if col0 else ncols, 512)
    j0 = col0 // tn
    return pl.pallas_call(
        _mm_kernel,
        grid=(m // tm, ncols // tn),
        in_specs=[pl.BlockSpec((tm, k), lambda i, j: (i, 0)),
                  pl.BlockSpec((k, tn), lambda i, j: (0, j0 + j))],
        out_specs=pl.BlockSpec((tm, tn), lambda i, j: (i, j)),
        out_shape=jax.ShapeDtypeStruct((m, ncols), BF16),
        compiler_params=_params(2),
    )(a, w)


def _dft_tables(n, scale):
    k = np.arange(n)
    ang = 2.0 * np.pi * ((k[:, None] * k[None, :]) % n) / n
    return np.cos(ang) * scale, np.sin(ang) * scale


def _fourier_chan_kernel(u_ref, t_ref, o_ref):
    dg = u_ref.shape[-1]
    res = _dot(u_ref[0], t_ref[...])
    o_ref[0, 0] = res[:, :dg].astype(o_ref.dtype)
    o_ref[0, 1] = res[:, dg:].astype(o_ref.dtype)


def _fourier_chan(z3, f_width):
    b, l, _ = z3.shape
    dg = f_width // F_GROUPS
    cd, sd = _dft_tables(dg, dg ** -0.5)
    table = jnp.asarray(np.concatenate([cd, sd], axis=1), dtype=BF16)
    tm = _tile(l, 1024)
    return pl.pallas_call(
        _fourier_chan_kernel,
        grid=(b, l // tm, F_GROUPS),
        in_specs=[pl.BlockSpec((1, tm, dg), lambda bi, i, g: (bi, i, g)),
                  pl.BlockSpec((dg, 2 * dg), lambda bi, i, g: (0, 0))],
        out_specs=pl.BlockSpec((1, 2, tm, dg), lambda bi, i, g: (bi, 0, i, g)),
        out_shape=jax.ShapeDtypeStruct((b, 2, l, f_width), BF16),
        compiler_params=_params(3),
    )(z3, table)


def _fourier_pos_kernel(t_ref, ab_ref, o_ref):
    o_ref[0] = _dot(t_ref[...], ab_ref[0]).astype(o_ref.dtype)


def _fourier_pos(ab):
    b, l2, fw = ab.shape
    l = l2 // 2
    cl, sl = _dft_tables(l, l ** -0.5)
    table = jnp.asarray(np.concatenate([cl, -sl], axis=1), dtype=BF16)
    tm = _tile(l, 512)
    tn = _tile(fw, 512)
    return pl.pallas_call(
        _fourier_pos_kernel,
        grid=(b, fw // tn, l // tm),
        in_specs=[pl.BlockSpec((tm, l2), lambda bi, j, i: (i, 0)),
                  pl.BlockSpec((1, l2, tn), lambda bi, j, i: (bi, 0, j))],
        out_specs=pl.BlockSpec((1, tm, tn), lambda bi, j, i: (bi, i, j)),
        out_shape=jax.ShapeDtypeStruct((b, l, fw), BF16),
        compiler_params=_params(3),
    )(table, ab)


def _na_geometry(rows):
    kr = min(NA_ROWS_MAX, rows)
    starts = [int(np.clip(r - kr // 2, 0, rows - kr)) - r + NA_ROWS_MAX - 1 for r in range(rows)]
    return kr, min(starts), max(starts) - min(starts) + 1


def _na_bias_slabs(rel_bias, rows):
    kr, min_start, n_slabs = _na_geometry(rows)
    cols = np.arange(GRID_W)
    win_start = np.clip(cols - NA_COLS // 2, 0, GRID_W - NA_COLS)
    rel_col = cols[None, :] - win_start[:, None]
    col_mask = (rel_col >= 0) & (rel_col < NA_COLS)
    dc_idx = np.clip(cols[None, :] - cols[:, None] + NA_COLS - 1, 0, 2 * NA_COLS - 2)
    neg = jnp.finfo(F32).min
    bias_c = jnp.where(col_mask[None, None], rel_bias.astype(F32)[:, :, dc_idx], neg)
    slabs = jnp.stack([bias_c[:, min_start + s:min_start + s + kr] for s in range(n_slabs)], axis=1)
    slabs = jnp.transpose(slabs, (0, 1, 3, 2, 4))
    h = rel_bias.shape[0]
    return slabs.reshape(h, n_slabs, GRID_W, kr * GRID_W)


def _na_kernel(q_ref, k_ref, v_ref, kc_ref, vc_ref, bias_ref, o_ref, s_ctx_all, p_ctx_all, o_loc_all,
               *, rows, rows_per_step, kr, min_start, scale):
    s_ctx_all[...] = lax.dot_general(q_ref[0], kc_ref[0], NT_DIMS, preferred_element_type=F32) * scale

    def scores(r):
        rs = jnp.clip(r - kr // 2, 0, rows - kr)
        slab = rs - r + (NA_ROWS_MAX - 1 - min_start)
        q0 = pl.multiple_of(r * GRID_W, GRID_W)
        k0 = pl.multiple_of(rs * GRID_W, GRID_W)
        qr = q_ref[0, pl.ds(q0, GRID_W), :]
        kw = k_ref[0, pl.ds(k0, kr * GRID_W), :]
        s_loc = lax.dot_general(qr, kw, NT_DIMS, preferred_element_type=F32) * scale + bias_ref[0, slab]
        return q0, k0, s_loc

    def softmax(q0, s_loc):
        s_ctx = s_ctx_all[pl.ds(q0, GRID_W), :]
        m = jnp.maximum(jnp.max(s_loc, axis=-1, keepdims=True), jnp.max(s_ctx, axis=-1, keepdims=True))
        p_loc = jnp.exp(s_loc - m)
        p_ctx = jnp.exp(s_ctx - m)
        inv = 1.0 / (jnp.sum(p_loc, axis=-1, keepdims=True) + jnp.sum(p_ctx, axis=-1, keepdims=True))
        return (p_loc * inv).astype(BF16), (p_ctx * inv).astype(BF16)

    def row_group(g, carry):
        sc = [scores(g * rows_per_step + u) for u in range(rows_per_step)]
        pr = [softmax(q0, s_loc) for q0, _, s_loc in sc]
        for (q0, k0, _), (p_loc, p_ctx) in zip(sc, pr):
            p_ctx_all[pl.ds(q0, GRID_W), :] = p_ctx
            o_loc_all[pl.ds(q0, GRID_W), :] = _dot(p_loc, v_ref[0, pl.ds(k0, kr * GRID_W), :])
        return carry

    lax.fori_loop(0, rows // rows_per_step, row_group, 0)
    o_ref[0] = (o_loc_all[...] + _dot(p_ctx_all[...], vc_ref[0])).astype(o_ref.dtype)


def _neighbourhood_attention(z3, zc3, bias_slabs, off_q, na_width):
    b, l, _ = z3.shape
    ctx_len = zc3.shape[1]
    dh = NA_HEAD_DIM
    heads = na_width // dh
    rows = l // GRID_W
    kr, min_start, n_slabs = _na_geometry(rows)
    qb, kb, vb = off_q // dh, (off_q + na_width) // dh, (off_q + 2 * na_width) // dh
    kern = functools.partial(_na_kernel, rows=rows, rows_per_step=_tile(rows, NA_ROWS_PER_STEP), kr=kr,
                             min_start=min_start, scale=dh ** -0.5)
    return pl.pallas_call(
        kern,
        grid=(heads, b),
        in_specs=[pl.BlockSpec((1, l, dh), lambda h, bi: (bi, 0, qb + h)),
                  pl.BlockSpec((1, l, dh), lambda h, bi: (bi, 0, kb + h)),
                  pl.BlockSpec((1, l, dh), lambda h, bi: (bi, 0, vb + h)),
                  pl.BlockSpec((1, ctx_len, dh), lambda h, bi: (bi, 0, h)),
                  pl.BlockSpec((1, ctx_len, dh), lambda h, bi: (bi, 0, heads + h)),
                  pl.BlockSpec((1, n_slabs, GRID_W, kr * GRID_W), lambda h, bi: (h, 0, 0, 0))],
        out_specs=pl.BlockSpec((1, l, dh), lambda h, bi: (bi, 0, h)),
        out_shape=jax.ShapeDtypeStruct((b, l, na_width), BF16),
        scratch_shapes=[pltpu.VMEM((l, ctx_len), F32), pltpu.VMEM((l, ctx_len), BF16),
                        pltpu.VMEM((l, dh), F32)],
        compiler_params=_params(2),
    )(z3, z3, z3, zc3, zc3, bias_slabs)


def _merge_kernel(yf_ref, on_ref, wf_ref, wn_ref, gf_ref, gn_ref, o_ref):
    y_f = _dot(yf_ref[...], wf_ref[...])
    y_n = _dot(on_ref[...], wn_ref[...])
    o_ref[...] = (gf_ref[...].astype(F32) * y_f + gn_ref[...].astype(F32) * y_n).astype(o_ref.dtype)


def _merge(y_four, o_na, w_f, w_n, z, off_gate):
    m, kf = y_four.shape
    kn = o_na.shape[1]
    d = w_f.shape[1]
    tm = _tile(m, 1024)
    tn = _tile(np.gcd(d, off_gate), 512)
    gf0, gn0 = off_gate // tn, (off_gate + d) // tn
    return pl.pallas_call(
        _merge_kernel,
        grid=(m // tm, d // tn),
        in_specs=[pl.BlockSpec((tm, kf), lambda i, j: (i, 0)),
                  pl.BlockSpec((tm, kn), lambda i, j: (i, 0)),
                  pl.BlockSpec((kf, tn), lambda i, j: (0, j)),
                  pl.BlockSpec((kn, tn), lambda i, j: (0, j)),
                  pl.BlockSpec((tm, tn), lambda i, j: (i, gf0 + j)),
                  pl.BlockSpec((tm, tn), lambda i, j: (i, gn0 + j))],
        out_specs=pl.BlockSpec((tm, tn), lambda i, j: (i, j)),
        out_shape=jax.ShapeDtypeStruct((m, d), BF16),
        compiler_params=_params(2),
    )(y_four, o_na, w_f, w_n, z, z)


def _out_proj_kernel(a_ref, w_ref, x_ref, gt_ref, o_ref):
    o_ref[...] = x_ref[...] + gt_ref[0] * _dot(a_ref[...], w_ref[...])


def _out_proj_residual(a, w, x2d, mod, k_gate, seq):
    m, k = a.shape
    d = w.shape[1]
    tm = _tile(seq, 1024)
    tn = _tile(d, 512)
    gate0 = k_gate * (d // tn)
    return pl.pallas_call(
        _out_proj_kernel,
        grid=(m // tm, d // tn),
        in_specs=[pl.BlockSpec((tm, k), lambda i, j: (i, 0)),
                  pl.BlockSpec((k, tn), lambda i, j: (0, j)),
                  pl.BlockSpec((tm, tn), lambda i, j: (i, j)),
                  pl.BlockSpec((1, 1, tn), lambda i, j: (i * tm // seq, 0, gate0 + j))],
        out_specs=pl.BlockSpec((tm, tn), lambda i, j: (i, j)),
        out_shape=jax.ShapeDtypeStruct((m, d), F32),
        compiler_params=_params(2),
    )(a, w, x2d, mod)


def _router_kernel(x_ref, g_ref, sc_ref, sh_ref, wr_ref, aff_ref):
    xn = _rms_modulate(x_ref[0], g_ref[...], sc_ref[0], sh_ref[0])
    x_hi = xn.astype(BF16)
    x_lo = (xn - x_hi.astype(F32)).astype(BF16)
    w = wr_ref[...]
    w_hi = w.astype(BF16)
    w_lo = (w - w_hi.astype(F32)).astype(BF16)
    logits = (lax.dot_general(w_hi, x_hi, NT_DIMS, preferred_element_type=F32)
              + lax.dot_general(w_lo, x_hi, NT_DIMS, preferred_element_type=F32)
              + lax.dot_general(w_hi, x_lo, NT_DIMS, preferred_element_type=F32))
    ex = jnp.exp(logits - jnp.max(logits, axis=0, keepdims=True))
    aff_ref[0] = ex / jnp.sum(ex, axis=0, keepdims=True)


def _router(x, gain, mod, k_shift, k_scale, w_router_t):
    b, l, d = x.shape
    e = w_router_t.shape[0]
    tm = _tile(l, 512)
    return pl.pallas_call(
        _router_kernel,
        grid=(b, l // tm),
        in_specs=[pl.BlockSpec((1, tm, d), lambda bi, i: (bi, i, 0)),
                  pl.BlockSpec((1, d), lambda bi, i: (0, 0)),
                  pl.BlockSpec((1, 1, d), lambda bi, i: (bi, 0, k_scale)),
                  pl.BlockSpec((1, 1, d), lambda bi, i: (bi, 0, k_shift)),
                  pl.BlockSpec((e, d), lambda bi, i: (0, 0))],
        out_specs=pl.BlockSpec((1, e, tm), lambda bi, i: (bi, 0, i)),
        out_shape=jax.ShapeDtypeStruct((b, e, l), F32),
        compiler_params=_params(2),
    )(x, gain.reshape(1, d), mod, mod, w_router_t)


def _select_kernel(a_ref, tri_ref, idx_ref, pos_ref, *, cap):
    a = a_ref[...]
    n_rows, l = a.shape
    capf = float(cap)

    def count(mask):
        return jnp.sum(jnp.where(mask, 1.0, 0.0), axis=-1, keepdims=True)

    def halve(_, lo_hi):
        lo, hi = lo_hi
        mid = 0.5 * lo + 0.5 * hi
        ok = count(a_ref[...] >= mid) >= capf
        return jnp.where(ok, mid, lo), jnp.where(ok, hi, mid)

    _, hi = lax.fori_loop(0, BISECT_STEPS, halve,
                          (jnp.zeros((n_rows, 1), F32), jnp.full((n_rows, 1), 2.0, F32)))
    tau = jnp.max(jnp.where(a < hi, a, -1.0), axis=-1, keepdims=True)
    above = a > tau
    tied = a == tau
    need = capf - count(above)
    tri = tri_ref[...]
    tied_before = _dot(jnp.where(tied, 1.0, 0.0).astype(BF16), tri)
    chosen = above | (tied & (tied_before < need))
    pos = _dot(jnp.where(chosen, 1.0, 0.0).astype(BF16), tri)
    pos_ref[...] = jnp.where(chosen, pos, -1.0)

    tok = lax.broadcasted_iota(jnp.int32, (cap, l), 1).astype(F32)
    slot = lax.broadcasted_iota(jnp.int32, (cap, l), 0).astype(F32)
    lane = lax.broadcasted_iota(jnp.int32, (cap, n_rows), 1)

    def compact(r, acc):
        hit = pos_ref[pl.ds(r, 1), :] == slot
        ids = jnp.sum(jnp.where(hit, tok, 0.0), axis=-1, keepdims=True)
        return jnp.where(lane == r, ids, acc)

    idx_ref[...] = lax.fori_loop(0, n_rows, compact, jnp.zeros((cap, n_rows), F32)).astype(jnp.int32)


def _select(aff_rows, cap):
    n_rows, l = aff_rows.shape
    tri = jnp.asarray(np.triu(np.ones((l, l), np.float32), k=1), dtype=BF16)
    return pl.pallas_call(
        functools.partial(_select_kernel, cap=cap),
        grid=(1,),
        in_specs=[pl.BlockSpec((n_rows, l), lambda i: (0, 0)),
                  pl.BlockSpec((l, l), lambda i: (0, 0))],
        out_specs=pl.BlockSpec((cap, n_rows), lambda i: (0, 0)),
        out_shape=jax.ShapeDtypeStruct((cap, n_rows), jnp.int32),
        scratch_shapes=[pltpu.VMEM((n_rows, l), F32)],
        compiler_params=_params(1),
    )(aff_rows, tri)


def _gather_copy(x_hbm, buf, sem, row, j):
    return pltpu.make_async_copy(x_hbm.at[pl.ds(row, 1)], buf.at[pl.ds(j, 1)], sem)


def _gather_norm_kernel(rows_ref, x_hbm, g_ref, sc_ref, sh_ref, o_ref, buf, sem, *, n_rows):
    i = pl.program_id(0)

    def start_rows(step, slot):
        def start(j, carry):
            _gather_copy(x_hbm, buf.at[slot], sem.at[slot], rows_ref[step * n_rows + j], j).start()
            return carry

        lax.fori_loop(0, n_rows, start, 0, unroll=8)

    @pl.when(i == 0)
    def _():
        start_rows(0, 0)

    @pl.when(i + 1 < pl.num_programs(0))
    def _():
        start_rows(i + 1, (i + 1) % 2)

    slot = i % 2

    def wait(j, carry):
        _gather_copy(x_hbm, buf.at[slot], sem.at[slot], 0, j).wait()
        return carry

    lax.fori_loop(0, n_rows, wait, 0, unroll=8)
    o_ref[...] = _rms_modulate(buf[slot], g_ref[...], sc_ref[0], sh_ref[0]).astype(o_ref.dtype)


def _gather_norm(flat_rows, x2d, gain, mod, k_shift, k_scale, cap, n_batch):
    n = flat_rows.shape[0]
    d = x2d.shape[1]
    r = _tile(cap, 256)
    per_b = cap // r
    bsel = lambda i, rows: (i // per_b) % n_batch
    grid_spec = pltpu.PrefetchScalarGridSpec(
        num_scalar_prefetch=1,
        grid=(n // r,),
        in_specs=[pl.BlockSpec(memory_space=pl.ANY),
                  pl.BlockSpec((1, d), lambda i, rows: (0, 0)),
                  pl.BlockSpec((1, 1, d), lambda i, rows: (bsel(i, rows), 0, k_scale)),
                  pl.BlockSpec((1, 1, d), lambda i, rows: (bsel(i, rows), 0, k_shift))],
        out_specs=pl.BlockSpec((r, d), lambda i, rows: (i, 0)),
        scratch_shapes=[pltpu.VMEM((2, r, d), F32), pltpu.SemaphoreType.DMA((2,))],
    )
    return pl.pallas_call(
        functools.partial(_gather_norm_kernel, n_rows=r),
        grid_spec=grid_spec,
        out_shape=jax.ShapeDtypeStruct((n, d), BF16),
        compiler_params=_params(1),
    )(flat_rows, x2d, gain.reshape(1, d), mod, mod)


def _ffn_up_kernel(x_ref, w1_ref, w3_ref, o_ref, w1_s, w3_s):
    @pl.when(pl.program_id(2) == 0)
    def _():
        w1_s[...] = w1_ref[0].astype(BF16)
        w3_s[...] = w3_ref[0].astype(BF16)

    xin = x_ref[...]
    a = _dot(xin, w1_s[...])
    o_ref[...] = (a * jax.nn.sigmoid(a) * _dot(xin, w3_s[...])).astype(o_ref.dtype)


def _ffn_up(xin, w1, w3):
    n, d = xin.shape
    e, _, ff = w1.shape
    rows_e = n // e
    tm = _tile(rows_e, 1024)
    tn = _tile(ff, 256)
    per_e = rows_e // tm
    return pl.pallas_call(
        _ffn_up_kernel,
        grid=(e, ff // tn, per_e),
        in_specs=[pl.BlockSpec((tm, d), lambda ei, j, i: (ei * per_e + i, 0)),
                  pl.BlockSpec((1, d, tn), lambda ei, j, i: (ei, 0, j)),
                  pl.BlockSpec((1, d, tn), lambda ei, j, i: (ei, 0, j))],
        out_specs=pl.BlockSpec((tm, tn), lambda ei, j, i: (ei * per_e + i, j)),
        out_shape=jax.ShapeDtypeStruct((n, ff), BF16),
        scratch_shapes=[pltpu.VMEM((d, tn), BF16), pltpu.VMEM((d, tn), BF16)],
        compiler_params=_params(3),
    )(xin, w1, w3)


def _ffn_down_kernel(h_ref, w2_ref, o_ref):
    o_ref[...] = _dot(h_ref[...], w2_ref[0].astype(BF16)).astype(o_ref.dtype)


def _ffn_down(hid, w2):
    n, ff = hid.shape
    e, _, d = w2.shape
    rows_e = n // e
    tn = _tile(d, 512)
    return pl.pallas_call(
        _ffn_down_kernel,
        grid=(e, d // tn),
        in_specs=[pl.BlockSpec((rows_e, ff), lambda ei, j: (ei, 0)),
                  pl.BlockSpec((1, ff, tn), lambda ei, j: (ei, 0, j))],
        out_specs=pl.BlockSpec((rows_e, tn), lambda ei, j: (ei, j)),
        out_shape=jax.ShapeDtypeStruct((n, d), BF16),
        compiler_params=_params(2),
    )(hid, w2)


def _combine_kernel(idx_ref, out_ref, aff_ref, x_ref, gt_ref, fg_ref, o_ref, *, n_experts, col_chunk):
    i, e = pl.program_id(1), pl.program_id(2)
    _, tm, d = o_ref.shape
    cap = idx_ref.shape[-1]

    @pl.when(e == 0)
    def _():
        o_ref[...] = jnp.zeros_like(o_ref)

    tok = lax.broadcasted_iota(jnp.int32, (tm, cap), 0) + i * tm
    onehot = jnp.where(tok == idx_ref[0, 0], 1.0, 0.0).astype(BF16)
    aff = aff_ref[0]
    lane = lax.broadcasted_iota(jnp.int32, aff.shape, 1)
    gate = jnp.sum(jnp.where(lane == e, aff, 0.0), axis=-1, keepdims=True)
    for c0 in range(0, d, col_chunk):
        cols = slice(c0, c0 + col_chunk)
        o_ref[0, :, cols] += gate * _dot(onehot, out_ref[0, 0, :, cols])

    @pl.when(e == n_experts - 1)
    def _():
        x2 = x_ref[0] + gt_ref[0] * o_ref[0]
        ms = jnp.mean(x2 * x2, axis=-1, keepdims=True)
        o_ref[0] = x2 * lax.rsqrt(ms + RMS_EPS) * fg_ref[...]


def _combine(idx, out, aff, x, mod, k_gate, final_g):
    b, l, d = x.shape
    e, cap = idx.shape[1], idx.shape[3]
    tm = _tile(l, 512)
    return pl.pallas_call(
        functools.partial(_combine_kernel, n_experts=e, col_chunk=_tile(d, 1024)),
        grid=(b, l // tm, e),
        in_specs=[pl.BlockSpec((1, 1, 1, cap), lambda bi, i, ei: (bi, ei, 0, 0)),
                  pl.BlockSpec((1, 1, cap, d), lambda bi, i, ei: (ei, bi, 0, 0)),
                  pl.BlockSpec((1, tm, e), lambda bi, i, ei: (bi, i, 0)),
                  pl.BlockSpec((1, tm, d), lambda bi, i, ei: (bi, i, 0)),
                  pl.BlockSpec((1, 1, d), lambda bi, i, ei: (bi, 0, k_gate)),
                  pl.BlockSpec((1, d), lambda bi, i, ei: (0, 0))],
        out_specs=pl.BlockSpec((1, tm, d), lambda bi, i, ei: (bi, i, 0)),
        out_shape=jax.ShapeDtypeStruct((b, l, d), F32),
        compiler_params=_params(3),
    )(idx, out, aff, x, mod, final_g.reshape(1, d))


def kernel(x, c, ctx, c_ctx, w_mod, b_mod, norm_mix_g, w_in, b_gate, w_fourier, na_rel_bias,
           w_na_out, w_out, norm_ffn_g, w_router, w1, w3, w2, final_norm_g):
    assert w_mod.shape[0] == 1, "single-layer stack only"
    b, l, d = x.shape
    ctx_len = ctx.shape[1]
    f_width = w_fourier.shape[1]
    na_width = w_na_out.shape[1]
    n_experts = w_router.shape[-1]
    in_width = w_in.shape[-1]
    off_q = f_width
    off_k = off_q + na_width
    off_g = off_k + 2 * na_width
    cap = EC_FACTOR * l // n_experts
    SH_M, SC_M, GT_M, SH_F, SC_F, GT_F = range(N_MOD)

    pad = (-(b + 1)) % 16
    cond = jnp.concatenate([c, c_ctx[None], jnp.zeros((pad, d), F32)], axis=0)
    mod_all = _ada_params(cond, w_mod[0], b_mod[0])
    mod = mod_all[:b].reshape(b, 1, N_MOD * d)
    mod_ctx = mod_all[b:b + 1].reshape(1, 1, N_MOD * d)

    w_in_bf = w_in[0].astype(BF16)
    xn = _norm_mod(x, norm_mix_g[0], mod, SH_M, SC_M)
    cn = _norm_mod(ctx, norm_mix_g[0], mod_ctx, SH_M, SC_M)
    b_full = jnp.concatenate([jnp.zeros((off_g,), F32), b_gate[0]]).reshape(1, in_width)
    z = _proj_in(xn.reshape(b * l, d), w_in_bf, b_full, off_g)
    zc = _matmul_cols(cn.reshape(b * ctx_len, d), w_in_bf, off_k, 2 * na_width)
    z3 = z.reshape(b, l, in_width)

    ab = _fourier_chan(z3, f_width)
    y_four = _fourier_pos(ab.reshape(b, 2 * l, f_width))
    bias_slabs = _na_bias_slabs(na_rel_bias[0], l // GRID_W)
    o_na = _neighbourhood_attention(z3, zc.reshape(b, ctx_len, 2 * na_width), bias_slabs, off_q, na_width)
    mixed = _merge(y_four.reshape(b * l, f_width), o_na.reshape(b * l, na_width),
                   w_fourier[0].astype(BF16), w_na_out[0].astype(BF16), z, off_g)
    x1 = _out_proj_residual(mixed, w_out[0].astype(BF16), x.reshape(b * l, d), mod, GT_M, l)

    x1_3 = x1.reshape(b, l, d)
    aff_t = _router(x1_3, norm_ffn_g[0], mod, SH_F, SC_F, w_router[0].T)
    idx_t = _select(aff_t.reshape(b * n_experts, l), cap)
    idx = idx_t.T.reshape(b, n_experts, cap)
    flat_rows = (jnp.transpose(idx, (1, 0, 2)) + (jnp.arange(b, dtype=jnp.int32) * l)[None, :, None]).reshape(-1)
    xin = _gather_norm(flat_rows, x1, norm_ffn_g[0], mod, SH_F, SC_F, cap, b)
    hid = _ffn_up(xin, w1[0], w3[0])
    out = _ffn_down(hid, w2[0])
    return _combine(idx.reshape(b, n_experts, 1, cap), out.reshape(n_experts, b, cap, d),
                    jnp.transpose(aff_t, (0, 2, 1)), x1_3, mod, GT_F, final_norm_g)
```

```python
import functools

import numpy as np
import jax
import jax.numpy as jnp
from jax import lax
from jax.experimental import pallas as pl
from jax.experimental.pallas import tpu as pltpu

GRID_W = 64
F_GROUPS = 4
NA_HEAD_DIM = 128
NA_ROWS_MAX = 8
NA_COLS = 16
EC_FACTOR = 2
N_MOD = 6
RMS_EPS = 1e-6

LANES = 128
VMEM_LIMIT_BYTES = 56 * 1024 * 1024
BISECT_STEPS = 160
NA_ROWS_PER_STEP = 8

F32 = jnp.float32
BF16 = jnp.bfloat16
NT_DIMS = (((1,), (1,)), ((), ()))


def _params(n_axes):
    return pltpu.CompilerParams(
        dimension_semantics=("arbitrary",) * n_axes, vmem_limit_bytes=VMEM_LIMIT_BYTES)


def _tile(dim, pref):
    t = min(dim, pref)
    while dim % t:
        t -= 1
    return t


def _dot(a, b):
    return jnp.dot(a, b, preferred_element_type=F32)


def _rms_modulate(x, gain, scale, shift):
    ms = jnp.mean(x * x, axis=-1, keepdims=True)
    y = x * lax.rsqrt(ms + RMS_EPS) * gain
    return y * (1.0 + scale) + shift


def _ada_kernel(c_ref, w_ref, b_ref, o_ref):
    cond = c_ref[...]
    act = cond * jax.nn.sigmoid(cond)
    o_ref[...] = _dot(act.astype(BF16), w_ref[...].astype(BF16)) + b_ref[...]


def _ada_params(cond, w_mod, b_mod):
    m, d = cond.shape
    n = w_mod.shape[1]
    tn = _tile(n, 512)
    return pl.pallas_call(
        _ada_kernel,
        grid=(n // tn,),
        in_specs=[pl.BlockSpec((m, d), lambda j: (0, 0)),
                  pl.BlockSpec((d, tn), lambda j: (0, j)),
                  pl.BlockSpec((1, tn), lambda j: (0, j))],
        out_specs=pl.BlockSpec((m, tn), lambda j: (0, j)),
        out_shape=jax.ShapeDtypeStruct((m, n), F32),
        compiler_params=_params(1),
    )(cond, w_mod, b_mod.reshape(1, n))


def _norm_mod_kernel(x_ref, g_ref, sc_ref, sh_ref, o_ref):
    o_ref[0] = _rms_modulate(x_ref[0], g_ref[...], sc_ref[0], sh_ref[0]).astype(o_ref.dtype)


def _norm_mod(x, gain, mod, k_shift, k_scale):
    b, l, d = x.shape
    per_batch = mod.shape[0] == b
    tm = _tile(l, 512)
    bsel = (lambda i: i) if per_batch else (lambda i: 0)
    return pl.pallas_call(
        _norm_mod_kernel,
        grid=(b, l // tm),
        in_specs=[pl.BlockSpec((1, tm, d), lambda bi, i: (bi, i, 0)),
                  pl.BlockSpec((1, d), lambda bi, i: (0, 0)),
                  pl.BlockSpec((1, 1, d), lambda bi, i: (bsel(bi), 0, k_scale)),
                  pl.BlockSpec((1, 1, d), lambda bi, i: (bsel(bi), 0, k_shift))],
        out_specs=pl.BlockSpec((1, tm, d), lambda bi, i: (bi, i, 0)),
        out_shape=jax.ShapeDtypeStruct((b, l, d), BF16),
        compiler_params=_params(2),
    )(x, gain.reshape(1, d), mod, mod)


def _proj_in_kernel(a_ref, w_ref, b_ref, o_ref, *, gate_block0):
    acc = _dot(a_ref[...], w_ref[...])
    j = pl.program_id(1)

    @pl.when(j < gate_block0)
    def _():
        o_ref[...] = acc.astype(o_ref.dtype)

    @pl.when(j >= gate_block0)
    def _():
        o_ref[...] = jax.nn.sigmoid(acc + b_ref[...]).astype(o_ref.dtype)


def _proj_in(a, w, b_full, off_gate):
    m, k = a.shape
    n = w.shape[1]
    tm = _tile(m, 1024)
    tn = _tile(np.gcd(n, off_gate), 1024)
    return pl.pallas_call(
        functools.partial(_proj_in_kernel, gate_block0=off_gate // tn),
        grid=(m // tm, n // tn),
        in_specs=[pl.BlockSpec((tm, k), lambda i, j: (i, 0)),
                  pl.BlockSpec((k, tn), lambda i, j: (0, j)),
                  pl.BlockSpec((1, tn), lambda i, j: (0, j))],
        out_specs=pl.BlockSpec((tm, tn), lambda i, j: (i, j)),
        out_shape=jax.ShapeDtypeStruct((m, n), BF16),
        compiler_params=_params(2),
    )(a, w, b_full)


def _mm_kernel(a_ref, w_ref, o_ref):
    o_ref[...] = _dot(a_ref[...], w_ref[...]).astype(o_ref.dtype)


def _matmul_cols(a, w, col0, ncols):
    m, k = a.shape
    tm = _tile(m, 1024)
    tn = _tile(np.gcd(ncols, col0) if col0 else ncols, 512)
    j0 = col0 // tn
    return pl.pallas_call(
        _mm_kernel,
        grid=(m // tm, ncols // tn),
        in_specs=[pl.BlockSpec((tm, k), lambda i, j: (i, 0)),
                  pl.BlockSpec((k, tn), lambda i, j: (0, j0 + j))],
        out_specs=pl.BlockSpec((tm, tn), lambda i, j: (i, j)),
        out_shape=jax.ShapeDtypeStruct((m, ncols), BF16),
        compiler_params=_params(2),
    )(a, w)


def _dft_tables(n, scale):
    k = np.arange(n)
    ang = 2.0 * np.pi * ((k[:, None] * k[None, :]) % n) / n
    return np.cos(ang) * scale, np.sin(ang) * scale


def _fourier_chan_kernel(u_ref, t_ref, o_ref):
    dg = u_ref.shape[-1]
    res = _dot(u_ref[0], t_ref[...])
    o_ref[0, 0] = res[:, :dg].astype(o_ref.dtype)
    o_ref[0, 1] = res[:, dg:].astype(o_ref.dtype)


def _fourier_chan(z3, f_width):
    b, l, _ = z3.shape
    dg = f_width // F_GROUPS
    cd, sd = _dft_tables(dg, dg ** -0.5)
    table = jnp.asarray(np.concatenate([cd, sd], axis=1), dtype=BF16)
    tm = _tile(l, 1024)
    return pl.pallas_call(
        _fourier_chan_kernel,
        grid=(b, l // tm, F_GROUPS),
        in_specs=[pl.BlockSpec((1, tm, dg), lambda bi, i, g: (bi, i, g)),
                  pl.BlockSpec((dg, 2 * dg), lambda bi, i, g: (0, 0))],
        out_specs=pl.BlockSpec((1, 2, tm, dg), lambda bi, i, g: (bi, 0, i, g)),
        out_shape=jax.ShapeDtypeStruct((b, 2, l, f_width), BF16),
        compiler_params=_params(3),
    )(z3, table)


def _fourier_pos_kernel(t_ref, ab_ref, o_ref):
    o_ref[0] = _dot(t_ref[...], ab_ref[0]).astype(o_ref.dtype)


def _fourier_pos(ab):
    b, l2, fw = ab.shape
    l = l2 // 2
    cl, sl = _dft_tables(l, l ** -0.5)
    table = jnp.asarray(np.concatenate([cl, -sl], axis=1), dtype=BF16)
    tm = _tile(l, 1024)
    tn = _tile(fw, 512)
    return pl.pallas_call(
        _fourier_pos_kernel,
        grid=(b, fw // tn, l // tm),
        in_specs=[pl.BlockSpec((tm, l2), lambda bi, j, i: (i, 0)),
                  pl.BlockSpec((1, l2, tn), lambda bi, j, i: (bi, 0, j))],
        out_specs=pl.BlockSpec((1, tm, tn), lambda bi, j, i: (bi, i, j)),
        out_shape=jax.ShapeDtypeStruct((b, l, fw), BF16),
        compiler_params=_params(3),
    )(table, ab)


def _na_geometry(rows):
    kr = min(NA_ROWS_MAX, rows)
    starts = [int(np.clip(r - kr // 2, 0, rows - kr)) - r + NA_ROWS_MAX - 1 for r in range(rows)]
    return kr, min(starts), max(starts) - min(starts) + 1


def _na_bias_slabs(rel_bias, rows):
    kr, min_start, n_slabs = _na_geometry(rows)
    cols = np.arange(GRID_W)
    win_start = np.clip(cols - NA_COLS // 2, 0, GRID_W - NA_COLS)
    rel_col = cols[None, :] - win_start[:, None]
    col_mask = (rel_col >= 0) & (rel_col < NA_COLS)
    dc_idx = np.clip(cols[None, :] - cols[:, None] + NA_COLS - 1, 0, 2 * NA_COLS - 2)
    neg = jnp.finfo(F32).min
    bias_c = jnp.where(col_mask[None, None], rel_bias.astype(F32)[:, :, dc_idx], neg)
    slabs = jnp.stack([bias_c[:, min_start + s:min_start + s + kr] for s in range(n_slabs)], axis=1)
    slabs = jnp.transpose(slabs, (0, 1, 3, 2, 4))
    h = rel_bias.shape[0]
    return slabs.reshape(h, n_slabs, GRID_W, kr * GRID_W)


def _na_kernel(q_ref, k_ref, v_ref, kc_ref, vc_ref, bias_ref, o_ref, s_ctx_all, p_ctx_all, o_loc_all,
               *, rows, rows_per_step, kr, min_start, scale):
    s_ctx_all[...] = lax.dot_general(q_ref[0], kc_ref[0], NT_DIMS, preferred_element_type=F32) * scale

    def scores(r):
        rs = jnp.clip(r - kr // 2, 0, rows - kr)
        slab = rs - r + (NA_ROWS_MAX - 1 - min_start)
        q0 = pl.multiple_of(r * GRID_W, GRID_W)
        k0 = pl.multiple_of(rs * GRID_W, GRID_W)
        qr = q_ref[0, pl.ds(q0, GRID_W), :]
        kw = k_ref[0, pl.ds(k0, kr * GRID_W), :]
        s_loc = lax.dot_general(qr, kw, NT_DIMS, preferred_element_type=F32) * scale + bias_ref[0, slab]
        return q0, k0, s_loc

    def softmax(q0, s_loc):
        s_ctx = s_ctx_all[pl.ds(q0, GRID_W), :]
        m = jnp.maximum(jnp.max(s_loc, axis=-1, keepdims=True), jnp.max(s_ctx, axis=-1, keepdims=True))
        p_loc = jnp.exp(s_loc - m)
        p_ctx = jnp.exp(s_ctx - m)
        inv = 1.0 / (jnp.sum(p_loc, axis=-1, keepdims=True) + jnp.sum(p_ctx, axis=-1, keepdims=True))
        return (p_loc * inv).astype(BF16), (p_ctx * inv).astype(BF16)

    def row_group(g, carry):
        sc = [scores(g * rows_per_step + u) for u in range(rows_per_step)]
        pr = [softmax(q0, s_loc) for q0, _, s_loc in sc]
        for (q0, k0, _), (p_loc, p_ctx) in zip(sc, pr):
            p_ctx_all[pl.ds(q0, GRID_W), :] = p_ctx
            o_loc_all[pl.ds(q0, GRID_W), :] = _dot(p_loc, v_ref[0, pl.ds(k0, kr * GRID_W), :])
        return carry

    lax.fori_loop(0, rows // rows_per_step, row_group, 0)
    o_ref[0] = (o_loc_all[...] + _dot(p_ctx_all[...], vc_ref[0])).astype(o_ref.dtype)


def _neighbourhood_attention(z3, zc3, bias_slabs, off_q, na_width):
    b, l, _ = z3.shape
    ctx_len = zc3.shape[1]
    dh = NA_HEAD_DIM
    heads = na_width // dh
    rows = l // GRID_W
    kr, min_start, n_slabs = _na_geometry(rows)
    qb, kb, vb = off_q // dh, (off_q + na_width) // dh, (off_q + 2 * na_width) // dh
    kern = functools.partial(_na_kernel, rows=rows, rows_per_step=_tile(rows, NA_ROWS_PER_STEP), kr=kr,
                             min_start=min_start, scale=dh ** -0.5)
    return pl.pallas_call(
        kern,
        grid=(heads, b),
        in_specs=[pl.BlockSpec((1, l, dh), lambda h, bi: (bi, 0, qb + h)),
                  pl.BlockSpec((1, l, dh), lambda h, bi: (bi, 0, kb + h)),
                  pl.BlockSpec((1, l, dh), lambda h, bi: (bi, 0, vb + h)),
                  pl.BlockSpec((1, ctx_len, dh), lambda h, bi: (bi, 0, h)),
                  pl.BlockSpec((1, ctx_len, dh), lambda h, bi: (bi, 0, heads + h)),
                  pl.BlockSpec((1, n_slabs, GRID_W, kr * GRID_W), lambda h, bi: (h, 0, 0, 0))],
        out_specs=pl.BlockSpec((1, l, dh), lambda h, bi: (bi, 0, h)),
        out_shape=jax.ShapeDtypeStruct((b, l, na_width), BF16),
        scratch_shapes=[pltpu.VMEM((l, ctx_len), F32), pltpu.VMEM((l, ctx_len), BF16),
                        pltpu.VMEM((l, dh), F32)],
        compiler_params=_params(2),
    )(z3, z3, z3, zc3, zc3, bias_slabs)


def _merge_kernel(yf_ref, on_ref, wf_ref, wn_ref, gf_ref, gn_ref, o_ref):
    y_f = _dot(yf_ref[...], wf_ref[...].astype(BF16))
    y_n = _dot(on_ref[...], wn_ref[...].astype(BF16))
    o_ref[...] = (gf_ref[...].astype(F32) * y_f + gn_ref[...].astype(F32) * y_n).astype(o_ref.dtype)


def _merge(y_four, o_na, w_f, w_n, z, off_gate):
    m, kf = y_four.shape
    kn = o_na.shape[1]
    d = w_f.shape[1]
    tm = _tile(m, 1024)
    tn = _tile(np.gcd(d, off_gate), 512)
    gf0, gn0 = off_gate // tn, (off_gate + d) // tn
    return pl.pallas_call(
        _merge_kernel,
        grid=(m // tm, d // tn),
        in_specs=[pl.BlockSpec((tm, kf), lambda i, j: (i, 0)),
                  pl.BlockSpec((tm, kn), lambda i, j: (i, 0)),
                  pl.BlockSpec((kf, tn), lambda i, j: (0, j)),
                  pl.BlockSpec((kn, tn), lambda i, j: (0, j)),
                  pl.BlockSpec((tm, tn), lambda i, j: (i, gf0 + j)),
                  pl.BlockSpec((tm, tn), lambda i, j: (i, gn0 + j))],
        out_specs=pl.BlockSpec((tm, tn), lambda i, j: (i, j)),
        out_shape=jax.ShapeDtypeStruct((m, d), BF16),
        compiler_params=_params(2),
    )(y_four, o_na, w_f, w_n, z, z)


def _out_proj_kernel(a_ref, w_ref, x_ref, gt_ref, o_ref):
    o_ref[...] = x_ref[...] + gt_ref[0] * _dot(a_ref[...], w_ref[...].astype(BF16))


def _out_proj_residual(a, w, x2d, mod, k_gate, seq):
    m, k = a.shape
    d = w.shape[1]
    tm = _tile(seq, 1024)
    tn = _tile(d, 512)
    gate0 = k_gate * (d // tn)
    return pl.pallas_call(
        _out_proj_kernel,
        grid=(m // tm, d // tn),
        in_specs=[pl.BlockSpec((tm, k), lambda i, j: (i, 0)),
                  pl.BlockSpec((k, tn), lambda i, j: (0, j)),
                  pl.BlockSpec((tm, tn), lambda i, j: (i, j)),
                  pl.BlockSpec((1, 1, tn), lambda i, j: (i * tm // seq, 0, gate0 + j))],
        out_specs=pl.BlockSpec((tm, tn), lambda i, j: (i, j)),
        out_shape=jax.ShapeDtypeStruct((m, d), F32),
        compiler_params=_params(2),
    )(a, w, x2d, mod)


def _router_kernel(x_ref, g_ref, sc_ref, sh_ref, wr_ref, aff_ref):
    xn = _rms_modulate(x_ref[0], g_ref[...], sc_ref[0], sh_ref[0])
    x_hi = xn.astype(BF16)
    x_lo = (xn - x_hi.astype(F32)).astype(BF16)
    w = wr_ref[...]
    w_hi = w.astype(BF16)
    w_lo = (w - w_hi.astype(F32)).astype(BF16)
    logits = (lax.dot_general(w_hi, x_hi, NT_DIMS, preferred_element_type=F32)
              + lax.dot_general(w_lo, x_hi, NT_DIMS, preferred_element_type=F32)
              + lax.dot_general(w_hi, x_lo, NT_DIMS, preferred_element_type=F32))
    ex = jnp.exp(logits - jnp.max(logits, axis=0, keepdims=True))
    aff_ref[0] = ex / jnp.sum(ex, axis=0, keepdims=True)


def _router(x, gain, mod, k_shift, k_scale, w_router_t):
    b, l, d = x.shape
    e = w_router_t.shape[0]
    tm = _tile(l, 512)
    return pl.pallas_call(
        _router_kernel,
        grid=(b, l // tm),
        in_specs=[pl.BlockSpec((1, tm, d), lambda bi, i: (bi, i, 0)),
                  pl.BlockSpec((1, d), lambda bi, i: (0, 0)),
                  pl.BlockSpec((1, 1, d), lambda bi, i: (bi, 0, k_scale)),
                  pl.BlockSpec((1, 1, d), lambda bi, i: (bi, 0, k_shift)),
                  pl.BlockSpec((e, d), lambda bi, i: (0, 0))],
        out_specs=pl.BlockSpec((1, e, tm), lambda bi, i: (bi, 0, i)),
        out_shape=jax.ShapeDtypeStruct((b, e, l), F32),
        compiler_params=_params(2),
    )(x, gain.reshape(1, d), mod, mod, w_router_t)


def _select_kernel(a_ref, tri_ref, idx_ref, pos_ref, *, cap):
    a = a_ref[...]
    n_rows, l = a.shape
    capf = float(cap)

    def count(mask):
        return jnp.sum(jnp.where(mask, 1.0, 0.0), axis=-1, keepdims=True)

    def halve(_, lo_hi):
        lo, hi = lo_hi
        mid = 0.5 * lo + 0.5 * hi
        ok = count(a_ref[...] >= mid) >= capf
        return jnp.where(ok, mid, lo), jnp.where(ok, hi, mid)

    _, hi = lax.fori_loop(0, BISECT_STEPS, halve,
                          (jnp.zeros((n_rows, 1), F32), jnp.full((n_rows, 1), 2.0, F32)))
    tau = jnp.max(jnp.where(a < hi, a, -1.0), axis=-1, keepdims=True)
    above = a > tau
    tied = a == tau
    need = capf - count(above)
    tri = tri_ref[...]
    tied_before = _dot(jnp.where(tied, 1.0, 0.0).astype(BF16), tri)
    chosen = above | (tied & (tied_before < need))
    pos = _dot(jnp.where(chosen, 1.0, 0.0).astype(BF16), tri)
    pos_ref[...] = jnp.where(chosen, pos, -1.0)

    tok = lax.broadcasted_iota(jnp.int32, (cap, l), 1).astype(F32)
    slot = lax.broadcasted_iota(jnp.int32, (cap, l), 0).astype(F32)
    lane = lax.broadcasted_iota(jnp.int32, (cap, n_rows), 1)

    def compact(r, acc):
        hit = pos_ref[pl.ds(r, 1), :] == slot
        ids = jnp.sum(jnp.where(hit, tok, 0.0), axis=-1, keepdims=True)
        return jnp.where(lane == r, ids, acc)

    idx_ref[...] = lax.fori_loop(0, n_rows, compact, jnp.zeros((cap, n_rows), F32)).astype(jnp.int32)


def _select(aff_rows, cap):
    n_rows, l = aff_rows.shape
    tri = jnp.asarray(np.triu(np.ones((l, l), np.float32), k=1), dtype=BF16)
    return pl.pallas_call(
        functools.partial(_select_kernel, cap=cap),
        grid=(1,),
        in_specs=[pl.BlockSpec((n_rows, l), lambda i: (0, 0)),
                  pl.BlockSpec((l, l), lambda i: (0, 0))],
        out_specs=pl.BlockSpec((cap, n_rows), lambda i: (0, 0)),
        out_shape=jax.ShapeDtypeStruct((cap, n_rows), jnp.int32),
        scratch_shapes=[pltpu.VMEM((n_rows, l), F32)],
        compiler_params=_params(1),
    )(aff_rows, tri)


def _gather_copy(x_hbm, buf, sem, row, j):
    return pltpu.make_async_copy(x_hbm.at[pl.ds(row, 1)], buf.at[pl.ds(j, 1)], sem)


def _gather_norm_kernel(rows_ref, x_hbm, g_ref, sc_ref, sh_ref, o_ref, buf, sem, *, n_rows):
    i = pl.program_id(0)

    def start_rows(step, slot):
        def start(j, carry):
            _gather_copy(x_hbm, buf.at[slot], sem.at[slot], rows_ref[step * n_rows + j], j).start()
            return carry

        lax.fori_loop(0, n_rows, start, 0, unroll=8)

    @pl.when(i == 0)
    def _():
        start_rows(0, 0)

    @pl.when(i + 1 < pl.num_programs(0))
    def _():
        start_rows(i + 1, (i + 1) % 2)

    slot = i % 2

    def wait(j, carry):
        _gather_copy(x_hbm, buf.at[slot], sem.at[slot], 0, j).wait()
        return carry

    lax.fori_loop(0, n_rows, wait, 0, unroll=8)
    o_ref[...] = _rms_modulate(buf[slot], g_ref[...], sc_ref[0], sh_ref[0]).astype(o_ref.dtype)


def _gather_norm(flat_rows, x2d, gain, mod, k_shift, k_scale, cap, n_batch):
    n = flat_rows.shape[0]
    d = x2d.shape[1]
    r = _tile(cap, 256)
    per_b = cap // r
    bsel = lambda i, rows: (i // per_b) % n_batch
    grid_spec = pltpu.PrefetchScalarGridSpec(
        num_scalar_prefetch=1,
        grid=(n // r,),
        in_specs=[pl.BlockSpec(memory_space=pl.ANY),
                  pl.BlockSpec((1, d), lambda i, rows: (0, 0)),
                  pl.BlockSpec((1, 1, d), lambda i, rows: (bsel(i, rows), 0, k_scale)),
                  pl.BlockSpec((1, 1, d), lambda i, rows: (bsel(i, rows), 0, k_shift))],
        out_specs=pl.BlockSpec((r, d), lambda i, rows: (i, 0)),
        scratch_shapes=[pltpu.VMEM((2, r, d), F32), pltpu.SemaphoreType.DMA((2,))],
    )
    return pl.pallas_call(
        functools.partial(_gather_norm_kernel, n_rows=r),
        grid_spec=grid_spec,
        out_shape=jax.ShapeDtypeStruct((n, d), BF16),
        compiler_params=_params(1),
    )(flat_rows, x2d, gain.reshape(1, d), mod, mod)


def _ffn_up_kernel(x_ref, w1_ref, w3_ref, o_ref):
    xin = x_ref[...]
    a = _dot(xin, w1_ref[0].astype(BF16))
    o_ref[...] = (a * jax.nn.sigmoid(a) * _dot(xin, w3_ref[0].astype(BF16))).astype(o_ref.dtype)


def _ffn_up(xin, w1, w3):
    n, d = xin.shape
    e, _, ff = w1.shape
    rows_e = n // e
    tm = _tile(rows_e, 1024)
    tn = _tile(ff, 256)
    per_e = rows_e // tm
    return pl.pallas_call(
        _ffn_up_kernel,
        grid=(e, per_e, ff // tn),
        in_specs=[pl.BlockSpec((tm, d), lambda ei, i, j: (ei * per_e + i, 0)),
                  pl.BlockSpec((1, d, tn), lambda ei, i, j: (ei, 0, j)),
                  pl.BlockSpec((1, d, tn), lambda ei, i, j: (ei, 0, j))],
        out_specs=pl.BlockSpec((tm, tn), lambda ei, i, j: (ei * per_e + i, j)),
        out_shape=jax.ShapeDtypeStruct((n, ff), BF16),
        compiler_params=_params(3),
    )(xin, w1, w3)


def _ffn_down_kernel(h_ref, w2_ref, o_ref):
    o_ref[...] = _dot(h_ref[...], w2_ref[0].astype(BF16)).astype(o_ref.dtype)


def _ffn_down(hid, w2):
    n, ff = hid.shape
    e, _, d = w2.shape
    rows_e = n // e
    tn = _tile(d, 512)
    return pl.pallas_call(
        _ffn_down_kernel,
        grid=(e, d // tn),
        in_specs=[pl.BlockSpec((rows_e, ff), lambda ei, j: (ei, 0)),
                  pl.BlockSpec((1, ff, tn), lambda ei, j: (ei, 0, j))],
        out_specs=pl.BlockSpec((rows_e, tn), lambda ei, j: (ei, j)),
        out_shape=jax.ShapeDtypeStruct((n, d), BF16),
        compiler_params=_params(2),
    )(hid, w2)


def _combine_kernel(idx_ref, out_ref, aff_ref, x_ref, gt_ref, fg_ref, o_ref, *, n_experts, col_chunk):
    i, e = pl.program_id(1), pl.program_id(2)
    _, tm, d = o_ref.shape
    cap = idx_ref.shape[-1]

    @pl.when(e == 0)
    def _():
        o_ref[...] = jnp.zeros_like(o_ref)

    tok = lax.broadcasted_iota(jnp.int32, (tm, cap), 0) + i * tm
    onehot = jnp.where(tok == idx_ref[0, 0], 1.0, 0.0).astype(BF16)
    aff = aff_ref[0]
    lane = lax.broadcasted_iota(jnp.int32, aff.shape, 1)
    gate = jnp.sum(jnp.where(lane == e, aff, 0.0), axis=-1, keepdims=True)
    for c0 in range(0, d, col_chunk):
        cols = slice(c0, c0 + col_chunk)
        o_ref[0, :, cols] += gate * _dot(onehot, out_ref[0, 0, :, cols])

    @pl.when(e == n_experts - 1)
    def _():
        x2 = x_ref[0] + gt_ref[0] * o_ref[0]
        ms = jnp.mean(x2 * x2, axis=-1, keepdims=True)
        o_ref[0] = x2 * lax.rsqrt(ms + RMS_EPS) * fg_ref[...]


def _combine(idx, out, aff, x, mod, k_gate, final_g):
    b, l, d = x.shape
    e, cap = idx.shape[1], idx.shape[3]
    tm = _tile(l, 512)
    return pl.pallas_call(
        functools.partial(_combine_kernel, n_experts=e, col_chunk=_tile(d, 1024)),
        grid=(b, l // tm, e),
        in_specs=[pl.BlockSpec((1, 1, 1, cap), lambda bi, i, ei: (bi, ei, 0, 0)),
                  pl.BlockSpec((1, 1, cap, d), lambda bi, i, ei: (ei, bi, 0, 0)),
                  pl.BlockSpec((1, tm, e), lambda bi, i, ei: (bi, i, 0)),
                  pl.BlockSpec((1, tm, d), lambda bi, i, ei: (bi, i, 0)),
                  pl.BlockSpec((1, 1, d), lambda bi, i, ei: (bi, 0, k_gate)),
                  pl.BlockSpec((1, d), lambda bi, i, ei: (0, 0))],
        out_specs=pl.BlockSpec((1, tm, d), lambda bi, i, ei: (bi, i, 0)),
        out_shape=jax.ShapeDtypeStruct((b, l, d), F32),
        compiler_params=_params(3),
    )(idx, out, aff, x, mod, final_g.reshape(1, d))


def kernel(x, c, ctx, c_ctx, w_mod, b_mod, norm_mix_g, w_in, b_gate, w_fourier, na_rel_bias,
           w_na_out, w_out, norm_ffn_g, w_router, w1, w3, w2, final_norm_g):
    assert w_mod.shape[0] == 1, "single-layer stack only"
    b, l, d = x.shape
    ctx_len = ctx.shape[1]
    f_width = w_fourier.shape[1]
    na_width = w_na_out.shape[1]
    n_experts = w_router.shape[-1]
    in_width = w_in.shape[-1]
    off_q = f_width
    off_k = off_q + na_width
    off_g = off_k + 2 * na_width
    cap = EC_FACTOR * l // n_experts
    SH_M, SC_M, GT_M, SH_F, SC_F, GT_F = range(N_MOD)

    pad = (-(b + 1)) % 16
    cond = jnp.concatenate([c, c_ctx[None], jnp.zeros((pad, d), F32)], axis=0)
    mod_all = _ada_params(cond, w_mod[0], b_mod[0])
    mod = mod_all[:b].reshape(b, 1, N_MOD * d)
    mod_ctx = mod_all[b:b + 1].reshape(1, 1, N_MOD * d)

    w_in_bf = w_in[0].astype(BF16)
    xn = _norm_mod(x, norm_mix_g[0], mod, SH_M, SC_M)
    cn = _norm_mod(ctx, norm_mix_g[0], mod_ctx, SH_M, SC_M)
    b_full = jnp.concatenate([jnp.zeros((off_g,), F32), b_gate[0]]).reshape(1, in_width)
    z = _proj_in(xn.reshape(b * l, d), w_in_bf, b_full, off_g)
    zc = _matmul_cols(cn.reshape(b * ctx_len, d), w_in_bf, off_k, 2 * na_width)
    z3 = z.reshape(b, l, in_width)

    ab = _fourier_chan(z3, f_width)
    y_four = _fourier_pos(ab.reshape(b, 2 * l, f_width))
    bias_slabs = _na_bias_slabs(na_rel_bias[0], l // GRID_W)
    o_na = _neighbourhood_attention(z3, zc.reshape(b, ctx_len, 2 * na_width), bias_slabs, off_q, na_width)
    mixed = _merge(y_four.reshape(b * l, f_width), o_na.reshape(b * l, na_width),
                   w_fourier[0], w_na_out[0], z, off_g)
    x1 = _out_proj_residual(mixed, w_out[0], x.reshape(b * l, d), mod, GT_M, l)

    x1_3 = x1.reshape(b, l, d)
    aff_t = _router(x1_3, norm_ffn_g[0], mod, SH_F, SC_F, w_router[0].T)
    idx_t = _select(aff_t.reshape(b * n_experts, l), cap)
    idx = idx_t.T.reshape(b, n_experts, cap)
    flat_rows = (jnp.transpose(idx, (1, 0, 2)) + (jnp.arange(b, dtype=jnp.int32) * l)[None, :, None]).reshape(-1)
    xin = _gather_norm(flat_rows, x1, norm_ffn_g[0], mod, SH_F, SC_F, cap, b)
    hid = _ffn_up(xin, w1[0], w3[0])
    out = _ffn_down(hid, w2[0])
    return _combine(idx.reshape(b, n_experts, 1, cap), out.reshape(n_experts, b, cap, d),
                    jnp.transpose(aff_t, (0, 2, 1)), x1_3, mod, GT_F, final_norm_g)
```

```python
import functools

import numpy as np
import jax
import jax.numpy as jnp
from jax import lax
from jax.experimental import pallas as pl
from jax.experimental.pallas import tpu as pltpu

GRID_W = 64
F_GROUPS = 4
NA_HEAD_DIM = 128
NA_ROWS_MAX = 8
NA_COLS = 16
EC_FACTOR = 2
N_MOD = 6
RMS_EPS = 1e-6

LANES = 128
SUBLANES = 8
COMBINE_CHUNK = 256
VMEM_LIMIT_BYTES = 56 * 1024 * 1024
BISECT_STEPS = 160
NA_ROWS_PER_STEP = 8

F32 = jnp.float32
BF16 = jnp.bfloat16
NT_DIMS = (((1,), (1,)), ((), ()))


def _params(n_axes):
    return pltpu.CompilerParams(
        dimension_semantics=("arbitrary",) * n_axes, vmem_limit_bytes=VMEM_LIMIT_BYTES)


def _tile(dim, pref):
    t = min(dim, pref)
    while dim % t:
        t -= 1
    return t


def _dot(a, b):
    return jnp.dot(a, b, preferred_element_type=F32)


def _rms_modulate(x, gain, scale, shift):
    ms = jnp.mean(x * x, axis=-1, keepdims=True)
    y = x * lax.rsqrt(ms + RMS_EPS) * gain
    return y * (1.0 + scale) + shift


def _ada_kernel(c_ref, w_ref, b_ref, o_ref):
    cond = c_ref[...]
    act = cond * jax.nn.sigmoid(cond)
    o_ref[...] = _dot(act.astype(BF16), w_ref[...].astype(BF16)) + b_ref[...]


def _ada_params(cond, w_mod, b_mod):
    m, d = cond.shape
    n = w_mod.shape[1]
    tn = _tile(n, 512)
    return pl.pallas_call(
        _ada_kernel,
        grid=(n // tn,),
        in_specs=[pl.BlockSpec((m, d), lambda j: (0, 0)),
                  pl.BlockSpec((d, tn), lambda j: (0, j)),
                  pl.BlockSpec((1, tn), lambda j: (0, j))],
        out_specs=pl.BlockSpec((m, tn), lambda j: (0, j)),
        out_shape=jax.ShapeDtypeStruct((m, n), F32),
        compiler_params=_params(1),
    )(cond, w_mod, b_mod.reshape(1, n))


def _norm_mod_kernel(x_ref, g_ref, sc_ref, sh_ref, o_ref):
    o_ref[0] = _rms_modulate(x_ref[0], g_ref[...], sc_ref[0], sh_ref[0]).astype(o_ref.dtype)


def _norm_mod(x, gain, mod, k_shift, k_scale):
    b, l, d = x.shape
    per_batch = mod.shape[0] == b
    tm = _tile(l, 512)
    bsel = (lambda i: i) if per_batch else (lambda i: 0)
    return pl.pallas_call(
        _norm_mod_kernel,
        grid=(b, l // tm),
        in_specs=[pl.BlockSpec((1, tm, d), lambda bi, i: (bi, i, 0)),
                  pl.BlockSpec((1, d), lambda bi, i: (0, 0)),
                  pl.BlockSpec((1, 1, d), lambda bi, i: (bsel(bi), 0, k_scale)),
                  pl.BlockSpec((1, 1, d), lambda bi, i: (bsel(bi), 0, k_shift))],
        out_specs=pl.BlockSpec((1, tm, d), lambda bi, i: (bi, i, 0)),
        out_shape=jax.ShapeDtypeStruct((b, l, d), BF16),
        compiler_params=_params(2),
    )(x, gain.reshape(1, d), mod, mod)


def _proj_in_kernel(a_ref, w_ref, b_ref, o_ref, *, gate_block0):
    acc = _dot(a_ref[...], w_ref[...])
    j = pl.program_id(1)

    @pl.when(j < gate_block0)
    def _():
        o_ref[...] = acc.astype(o_ref.dtype)

    @pl.when(j >= gate_block0)
    def _():
        o_ref[...] = jax.nn.sigmoid(acc + b_ref[...]).astype(o_ref.dtype)


def _proj_in(a, w, b_full, off_gate):
    m, k = a.shape
    n = w.shape[1]
    tm = _tile(m, 1024)
    tn = _tile(np.gcd(n, off_gate), 1024)
    return pl.pallas_call(
        functools.partial(_proj_in_kernel, gate_block0=off_gate // tn),
        grid=(m // tm, n // tn),
        in_specs=[pl.BlockSpec((tm, k), lambda i, j: (i, 0)),
                  pl.BlockSpec((k, tn), lambda i, j: (0, j)),
                  pl.BlockSpec((1, tn), lambda i, j: (0, j))],
        out_specs=pl.BlockSpec((tm, tn), lambda i, j: (i, j)),
        out_shape=jax.ShapeDtypeStruct((m, n), BF16),
        compiler_params=_params(2),
    )(a, w, b_full)


def _mm_kernel(a_ref, w_ref, o_ref):
    o_ref[...] = _dot(a_ref[...], w_ref[...]).astype(o_ref.dtype)


def _matmul_cols(a, w, col0, ncols):
    m, k = a.shape
    tm = _tile(m, 1024)
    tn = _tile(np.gcd(ncols, col0) if col0 else ncols, 512)
    j0 = col0 // tn
    return pl.pallas_call(
        _mm_kernel,
        grid=(m // tm, ncols // tn),
        in_specs=[pl.BlockSpec((tm, k), lambda i, j: (i, 0)),
                  pl.BlockSpec((k, tn), lambda i, j: (0, j0 + j))],
        out_specs=pl.BlockSpec((tm, tn), lambda i, j: (i, j)),
        out_shape=jax.ShapeDtypeStruct((m, ncols), BF16),
        compiler_params=_params(2),
    )(a, w)


def _dft_tables(n, scale):
    k = np.arange(n)
    ang = 2.0 * np.pi * ((k[:, None] * k[None, :]) % n) / n
    return np.cos(ang) * scale, np.sin(ang) * scale


def _fourier_chan_kernel(u_ref, t_ref, o_ref):
    dg = u_ref.shape[-1]
    res = _dot(u_ref[0], t_ref[...])
    o_ref[0, 0] = res[:, :dg].astype(o_ref.dtype)
    o_ref[0, 1] = res[:, dg:].astype(o_ref.dtype)


def _fourier_chan(z3, f_width):
    b, l, _ = z3.shape
    dg = f_width // F_GROUPS
    cd, sd = _dft_tables(dg, dg ** -0.5)
    table = jnp.asarray(np.concatenate([cd, sd], axis=1), dtype=BF16)
    tm = _tile(l, 1024)
    return pl.pallas_call(
        _fourier_chan_kernel,
        grid=(b, l // tm, F_GROUPS),
        in_specs=[pl.BlockSpec((1, tm, dg), lambda bi, i, g: (bi, i, g)),
                  pl.BlockSpec((dg, 2 * dg), lambda bi, i, g: (0, 0))],
        out_specs=pl.BlockSpec((1, 2, tm, dg), lambda bi, i, g: (bi, 0, i, g)),
        out_shape=jax.ShapeDtypeStruct((b, 2, l, f_width), BF16),
        compiler_params=_params(3),
    )(z3, table)


def _fourier_pos_kernel(t_ref, ab_ref, o_ref):
    o_ref[0] = _dot(t_ref[...], ab_ref[0]).astype(o_ref.dtype)


def _fourier_pos(ab):
    b, l2, fw = ab.shape
    l = l2 // 2
    cl, sl = _dft_tables(l, l ** -0.5)
    table = jnp.asarray(np.concatenate([cl, -sl], axis=1), dtype=BF16)
    tm = _tile(l, 1024)
    tn = _tile(fw, 512)
    return pl.pallas_call(
        _fourier_pos_kernel,
        grid=(b, fw // tn, l // tm),
        in_specs=[pl.BlockSpec((tm, l2), lambda bi, j, i: (i, 0)),
                  pl.BlockSpec((1, l2, tn), lambda bi, j, i: (bi, 0, j))],
        out_specs=pl.BlockSpec((1, tm, tn), lambda bi, j, i: (bi, i, j)),
        out_shape=jax.ShapeDtypeStruct((b, l, fw), BF16),
        compiler_params=_params(3),
    )(table, ab)


def _na_geometry(rows):
    kr = min(NA_ROWS_MAX, rows)
    starts = [int(np.clip(r - kr // 2, 0, rows - kr)) - r + NA_ROWS_MAX - 1 for r in range(rows)]
    return kr, min(starts), max(starts) - min(starts) + 1


def _na_bias_slabs(rel_bias, rows):
    kr, min_start, n_slabs = _na_geometry(rows)
    cols = np.arange(GRID_W)
    win_start = np.clip(cols - NA_COLS // 2, 0, GRID_W - NA_COLS)
    rel_col = cols[None, :] - win_start[:, None]
    col_mask = (rel_col >= 0) & (rel_col < NA_COLS)
    dc_idx = np.clip(cols[None, :] - cols[:, None] + NA_COLS - 1, 0, 2 * NA_COLS - 2)
    neg = jnp.finfo(F32).min
    bias_c = jnp.where(col_mask[None, None], rel_bias.astype(F32)[:, :, dc_idx], neg)
    slabs = jnp.stack([bias_c[:, min_start + s:min_start + s + kr] for s in range(n_slabs)], axis=1)
    slabs = jnp.transpose(slabs, (0, 1, 3, 2, 4))
    h = rel_bias.shape[0]
    return slabs.reshape(h, n_slabs, GRID_W, kr * GRID_W)


def _na_kernel(q_ref, k_ref, v_ref, kc_ref, vc_ref, bias_ref, o_ref, s_ctx_all, p_ctx_all, o_loc_all,
               *, rows, rows_per_step, kr, min_start, scale):
    s_ctx_all[...] = lax.dot_general(q_ref[0], kc_ref[0], NT_DIMS, preferred_element_type=F32) * scale

    def scores(r):
        rs = jnp.clip(r - kr // 2, 0, rows - kr)
        slab = rs - r + (NA_ROWS_MAX - 1 - min_start)
        q0 = pl.multiple_of(r * GRID_W, GRID_W)
        k0 = pl.multiple_of(rs * GRID_W, GRID_W)
        qr = q_ref[0, pl.ds(q0, GRID_W), :]
        kw = k_ref[0, pl.ds(k0, kr * GRID_W), :]
        s_loc = lax.dot_general(qr, kw, NT_DIMS, preferred_element_type=F32) * scale + bias_ref[0, slab]
        return q0, k0, s_loc

    def softmax(q0, s_loc):
        s_ctx = s_ctx_all[pl.ds(q0, GRID_W), :]
        m = jnp.maximum(jnp.max(s_loc, axis=-1, keepdims=True), jnp.max(s_ctx, axis=-1, keepdims=True))
        p_loc = jnp.exp(s_loc - m)
        p_ctx = jnp.exp(s_ctx - m)
        inv = 1.0 / (jnp.sum(p_loc, axis=-1, keepdims=True) + jnp.sum(p_ctx, axis=-1, keepdims=True))
        return (p_loc * inv).astype(BF16), (p_ctx * inv).astype(BF16)

    def row_group(g, carry):
        sc = [scores(g * rows_per_step + u) for u in range(rows_per_step)]
        pr = [softmax(q0, s_loc) for q0, _, s_loc in sc]
        for (q0, k0, _), (p_loc, p_ctx) in zip(sc, pr):
            p_ctx_all[pl.ds(q0, GRID_W), :] = p_ctx
            o_loc_all[pl.ds(q0, GRID_W), :] = _dot(p_loc, v_ref[0, pl.ds(k0, kr * GRID_W), :])
        return carry

    lax.fori_loop(0, rows // rows_per_step, row_group, 0)
    o_ref[0] = (o_loc_all[...] + _dot(p_ctx_all[...], vc_ref[0])).astype(o_ref.dtype)


def _neighbourhood_attention(z3, zc3, bias_slabs, off_q, na_width):
    b, l, _ = z3.shape
    ctx_len = zc3.shape[1]
    dh = NA_HEAD_DIM
    heads = na_width // dh
    rows = l // GRID_W
    kr, min_start, n_slabs = _na_geometry(rows)
    qb, kb, vb = off_q // dh, (off_q + na_width) // dh, (off_q + 2 * na_width) // dh
    kern = functools.partial(_na_kernel, rows=rows, rows_per_step=_tile(rows, NA_ROWS_PER_STEP), kr=kr,
                             min_start=min_start, scale=dh ** -0.5)
    return pl.pallas_call(
        kern,
        grid=(heads, b),
        in_specs=[pl.BlockSpec((1, l, dh), lambda h, bi: (bi, 0, qb + h)),
                  pl.BlockSpec((1, l, dh), lambda h, bi: (bi, 0, kb + h)),
                  pl.BlockSpec((1, l, dh), lambda h, bi: (bi, 0, vb + h)),
                  pl.BlockSpec((1, ctx_len, dh), lambda h, bi: (bi, 0, h)),
                  pl.BlockSpec((1, ctx_len, dh), lambda h, bi: (bi, 0, heads + h)),
                  pl.BlockSpec((1, n_slabs, GRID_W, kr * GRID_W), lambda h, bi: (h, 0, 0, 0))],
        out_specs=pl.BlockSpec((1, l, dh), lambda h, bi: (bi, 0, h)),
        out_shape=jax.ShapeDtypeStruct((b, l, na_width), BF16),
        scratch_shapes=[pltpu.VMEM((l, ctx_len), F32), pltpu.VMEM((l, ctx_len), BF16),
                        pltpu.VMEM((l, dh), F32)],
        compiler_params=_params(2),
    )(z3, z3, z3, zc3, zc3, bias_slabs)


def _merge_kernel(yf_ref, on_ref, wf_ref, wn_ref, gf_ref, gn_ref, o_ref):
    y_f = _dot(yf_ref[...], wf_ref[...].astype(BF16))
    y_n = _dot(on_ref[...], wn_ref[...].astype(BF16))
    o_ref[...] = (gf_ref[...].astype(F32) * y_f + gn_ref[...].astype(F32) * y_n).astype(o_ref.dtype)


def _merge(y_four, o_na, w_f, w_n, z, off_gate):
    m, kf = y_four.shape
    kn = o_na.shape[1]
    d = w_f.shape[1]
    tm = _tile(m, 1024)
    tn = _tile(np.gcd(d, off_gate), 512)
    gf0, gn0 = off_gate // tn, (off_gate + d) // tn
    return pl.pallas_call(
        _merge_kernel,
        grid=(m // tm, d // tn),
        in_specs=[pl.BlockSpec((tm, kf), lambda i, j: (i, 0)),
                  pl.BlockSpec((tm, kn), lambda i, j: (i, 0)),
                  pl.BlockSpec((kf, tn), lambda i, j: (0, j)),
                  pl.BlockSpec((kn, tn), lambda i, j: (0, j)),
                  pl.BlockSpec((tm, tn), lambda i, j: (i, gf0 + j)),
                  pl.BlockSpec((tm, tn), lambda i, j: (i, gn0 + j))],
        out_specs=pl.BlockSpec((tm, tn), lambda i, j: (i, j)),
        out_shape=jax.ShapeDtypeStruct((m, d), BF16),
        compiler_params=_params(2),
    )(y_four, o_na, w_f, w_n, z, z)


def _out_proj_kernel(a_ref, w_ref, x_ref, gt_ref, o_ref):
    o_ref[...] = x_ref[...] + gt_ref[0] * _dot(a_ref[...], w_ref[...].astype(BF16))


def _out_proj_residual(a, w, x2d, mod, k_gate, seq):
    m, k = a.shape
    d = w.shape[1]
    tm = _tile(seq, 1024)
    tn = _tile(d, 512)
    gate0 = k_gate * (d // tn)
    return pl.pallas_call(
        _out_proj_kernel,
        grid=(m // tm, d // tn),
        in_specs=[pl.BlockSpec((tm, k), lambda i, j: (i, 0)),
                  pl.BlockSpec((k, tn), lambda i, j: (0, j)),
                  pl.BlockSpec((tm, tn), lambda i, j: (i, j)),
                  pl.BlockSpec((1, 1, tn), lambda i, j: (i * tm // seq, 0, gate0 + j))],
        out_specs=pl.BlockSpec((tm, tn), lambda i, j: (i, j)),
        out_shape=jax.ShapeDtypeStruct((m, d), F32),
        compiler_params=_params(2),
    )(a, w, x2d, mod)


def _router_kernel(x_ref, g_ref, sc_ref, sh_ref, wr_ref, aff_ref):
    xn = _rms_modulate(x_ref[0], g_ref[...], sc_ref[0], sh_ref[0])
    x_hi = xn.astype(BF16)
    x_lo = (xn - x_hi.astype(F32)).astype(BF16)
    w = wr_ref[...]
    w_hi = w.astype(BF16)
    w_lo = (w - w_hi.astype(F32)).astype(BF16)
    logits = (lax.dot_general(w_hi, x_hi, NT_DIMS, preferred_element_type=F32)
              + lax.dot_general(w_lo, x_hi, NT_DIMS, preferred_element_type=F32)
              + lax.dot_general(w_hi, x_lo, NT_DIMS, preferred_element_type=F32))
    ex = jnp.exp(logits - jnp.max(logits, axis=0, keepdims=True))
    aff_ref[0] = ex / jnp.sum(ex, axis=0, keepdims=True)


def _router(x, gain, mod, k_shift, k_scale, w_router_t):
    b, l, d = x.shape
    e = w_router_t.shape[0]
    tm = _tile(l, 512)
    return pl.pallas_call(
        _router_kernel,
        grid=(b, l // tm),
        in_specs=[pl.BlockSpec((1, tm, d), lambda bi, i: (bi, i, 0)),
                  pl.BlockSpec((1, d), lambda bi, i: (0, 0)),
                  pl.BlockSpec((1, 1, d), lambda bi, i: (bi, 0, k_scale)),
                  pl.BlockSpec((1, 1, d), lambda bi, i: (bi, 0, k_shift)),
                  pl.BlockSpec((e, d), lambda bi, i: (0, 0))],
        out_specs=pl.BlockSpec((1, e, tm), lambda bi, i: (bi, 0, i)),
        out_shape=jax.ShapeDtypeStruct((b, e, l), F32),
        compiler_params=_params(2),
    )(x, gain.reshape(1, d), mod, mod, w_router_t)


def _select_kernel(a_ref, tri_ref, same_ref, lower_ref, idx_ref, dst_ref, gate_ref, cum_ref, pos_ref, rank_ref,
                   *, cap):
    a = a_ref[...]
    n_rows, l = a.shape
    capf = float(cap)

    def count(mask):
        return jnp.sum(jnp.where(mask, 1.0, 0.0), axis=-1, keepdims=True)

    def halve(_, lo_hi):
        lo, hi = lo_hi
        mid = 0.5 * lo + 0.5 * hi
        ok = count(a_ref[...] >= mid) >= capf
        return jnp.where(ok, mid, lo), jnp.where(ok, hi, mid)

    _, hi = lax.fori_loop(0, BISECT_STEPS, halve,
                          (jnp.zeros((n_rows, 1), F32), jnp.full((n_rows, 1), 2.0, F32)))
    tau = jnp.max(jnp.where(a < hi, a, -1.0), axis=-1, keepdims=True)
    above = a > tau
    tied = a == tau
    need = capf - count(above)
    tri = tri_ref[...]
    tied_before = _dot(jnp.where(tied, 1.0, 0.0).astype(BF16), tri)
    chosen = above | (tied & (tied_before < need))
    chosen_b = jnp.where(chosen, 1.0, 0.0).astype(BF16)
    pos = _dot(chosen_b, tri)
    pos_ref[...] = jnp.where(chosen, pos, -1.0)
    per_token = _dot(same_ref[...], chosen_b)
    before_token = _dot(per_token.astype(BF16), tri)
    cum_ref[...] = before_token
    rank_ref[...] = before_token + _dot(lower_ref[...], chosen_b)

    tok = lax.broadcasted_iota(jnp.int32, (cap, l), 1).astype(F32)
    slot = lax.broadcasted_iota(jnp.int32, (cap, l), 0).astype(F32)
    lane = lax.broadcasted_iota(jnp.int32, (cap, n_rows), 1)

    def compact(r, accs):
        hit = pos_ref[pl.ds(r, 1), :] == slot
        picked = [jnp.sum(jnp.where(hit, v, 0.0), axis=-1, keepdims=True)
                  for v in (tok, rank_ref[pl.ds(r, 1), :], a_ref[pl.ds(r, 1), :])]
        return tuple(jnp.where(lane == r, p, acc) for p, acc in zip(picked, accs))

    zeros = jnp.zeros((cap, n_rows), F32)
    ids, ranks, gates = lax.fori_loop(0, n_rows, compact, (zeros, zeros, zeros))
    idx_ref[...] = ids.astype(jnp.int32)
    dst_ref[...] = ranks.astype(jnp.int32)
    gate_ref[...] = gates


def _select(aff_rows, cap, n_experts):
    n_rows, l = aff_rows.shape
    tri = jnp.asarray(np.triu(np.ones((l, l), np.float32), k=1), dtype=BF16)
    group = np.arange(n_rows) // n_experts
    same = group[:, None] == group[None, :]
    lower = same & (np.arange(n_rows)[None, :] < np.arange(n_rows)[:, None])
    full = lambda shape: pl.BlockSpec(shape, lambda i: (0, 0))
    slot_major = jax.ShapeDtypeStruct((cap, n_rows), jnp.int32)
    return pl.pallas_call(
        functools.partial(_select_kernel, cap=cap),
        grid=(1,),
        in_specs=[full((n_rows, l)), full((l, l)), full((n_rows, n_rows)), full((n_rows, n_rows))],
        out_specs=[full((cap, n_rows)), full((cap, n_rows)), full((cap, n_rows)), full((n_rows, l))],
        out_shape=[slot_major, slot_major, jax.ShapeDtypeStruct((cap, n_rows), F32),
                   jax.ShapeDtypeStruct((n_rows, l), F32)],
        scratch_shapes=[pltpu.VMEM((n_rows, l), F32), pltpu.VMEM((n_rows, l), F32)],
        compiler_params=_params(1),
    )(aff_rows, tri, jnp.asarray(same, dtype=BF16), jnp.asarray(lower, dtype=BF16))


def _gather_copy(x_hbm, buf, sem, row, j):
    return pltpu.make_async_copy(x_hbm.at[pl.ds(row, 1)], buf.at[pl.ds(j, 1)], sem)


def _gather_norm_kernel(rows_ref, x_hbm, g_ref, sc_ref, sh_ref, o_ref, buf, sem, *, n_rows):
    i = pl.program_id(0)

    def start_rows(step, slot):
        def start(j, carry):
            _gather_copy(x_hbm, buf.at[slot], sem.at[slot], rows_ref[step * n_rows + j], j).start()
            return carry

        lax.fori_loop(0, n_rows, start, 0, unroll=8)

    @pl.when(i == 0)
    def _():
        start_rows(0, 0)

    @pl.when(i + 1 < pl.num_programs(0))
    def _():
        start_rows(i + 1, (i + 1) % 2)

    slot = i % 2

    def wait(j, carry):
        _gather_copy(x_hbm, buf.at[slot], sem.at[slot], 0, j).wait()
        return carry

    lax.fori_loop(0, n_rows, wait, 0, unroll=8)
    o_ref[...] = _rms_modulate(buf[slot], g_ref[...], sc_ref[0], sh_ref[0]).astype(o_ref.dtype)


def _gather_norm(flat_rows, x2d, gain, mod, k_shift, k_scale, cap, n_batch):
    n = flat_rows.shape[0]
    d = x2d.shape[1]
    r = _tile(cap, 256)
    per_b = cap // r
    bsel = lambda i, rows: (i // per_b) % n_batch
    grid_spec = pltpu.PrefetchScalarGridSpec(
        num_scalar_prefetch=1,
        grid=(n // r,),
        in_specs=[pl.BlockSpec(memory_space=pl.ANY),
                  pl.BlockSpec((1, d), lambda i, rows: (0, 0)),
                  pl.BlockSpec((1, 1, d), lambda i, rows: (bsel(i, rows), 0, k_scale)),
                  pl.BlockSpec((1, 1, d), lambda i, rows: (bsel(i, rows), 0, k_shift))],
        out_specs=pl.BlockSpec((r, d), lambda i, rows: (i, 0)),
        scratch_shapes=[pltpu.VMEM((2, r, d), F32), pltpu.SemaphoreType.DMA((2,))],
    )
    return pl.pallas_call(
        functools.partial(_gather_norm_kernel, n_rows=r),
        grid_spec=grid_spec,
        out_shape=jax.ShapeDtypeStruct((n, d), BF16),
        compiler_params=_params(1),
    )(flat_rows, x2d, gain.reshape(1, d), mod, mod)


def _ffn_up_kernel(x_ref, w1_ref, w3_ref, o_ref):
    xin = x_ref[...]
    a = _dot(xin, w1_ref[0].astype(BF16))
    o_ref[...] = (a * jax.nn.sigmoid(a) * _dot(xin, w3_ref[0].astype(BF16))).astype(o_ref.dtype)


def _ffn_up(xin, w1, w3):
    n, d = xin.shape
    e, _, ff = w1.shape
    rows_e = n // e
    tm = _tile(rows_e, 1024)
    tn = _tile(ff, 256)
    per_e = rows_e // tm
    return pl.pallas_call(
        _ffn_up_kernel,
        grid=(e, per_e, ff // tn),
        in_specs=[pl.BlockSpec((tm, d), lambda ei, i, j: (ei * per_e + i, 0)),
                  pl.BlockSpec((1, d, tn), lambda ei, i, j: (ei, 0, j)),
                  pl.BlockSpec((1, d, tn), lambda ei, i, j: (ei, 0, j))],
        out_specs=pl.BlockSpec((tm, tn), lambda ei, i, j: (ei * per_e + i, j)),
        out_shape=jax.ShapeDtypeStruct((n, ff), BF16),
        compiler_params=_params(3),
    )(xin, w1, w3)


def _bf16_bits(x):
    return lax.bitcast_convert_type(x.astype(BF16).astype(F32), jnp.uint32)


def _pack_bf16_pair(lo, hi):
    return (_bf16_bits(lo) >> 16) | _bf16_bits(hi)


def _unpack_bf16_pair(words):
    lo = lax.bitcast_convert_type(words << 16, F32).astype(BF16)
    hi = lax.bitcast_convert_type(words & jnp.uint32(0xFFFF0000), F32).astype(BF16)
    return lo, hi


def _ffn_down_kernel(h_ref, wa_ref, wb_ref, g_ref, o_ref):
    h = h_ref[...]
    gate = g_ref[...]
    o_ref[...] = _pack_bf16_pair(_dot(h, wa_ref[0].astype(BF16)) * gate,
                                 _dot(h, wb_ref[0].astype(BF16)) * gate)


def _ffn_down(hid, w2, gate_col):
    n, ff = hid.shape
    e, _, d = w2.shape
    rows_e = n // e
    half = d // 2
    tw = _tile(half, 256)
    nb = half // tw
    return pl.pallas_call(
        _ffn_down_kernel,
        grid=(e, nb),
        in_specs=[pl.BlockSpec((rows_e, ff), lambda ei, j: (ei, 0)),
                  pl.BlockSpec((1, ff, tw), lambda ei, j: (ei, 0, j)),
                  pl.BlockSpec((1, ff, tw), lambda ei, j: (ei, 0, nb + j)),
                  pl.BlockSpec((rows_e, 1), lambda ei, j: (ei, 0))],
        out_specs=pl.BlockSpec((rows_e, tw), lambda ei, j: (ei, j)),
        out_shape=jax.ShapeDtypeStruct((n, half), jnp.uint32),
        compiler_params=_params(2),
    )(hid, w2, w2, gate_col)


def _row_copy(src, dst_hbm, sem, j, row):
    return pltpu.make_async_copy(src.at[pl.ds(j, 1)], dst_hbm.at[pl.ds(row, 1)], sem)


def _permute_kernel(dest_ref, src_ref, dst_hbm, sem):
    n_rows = src_ref.shape[0]
    base = pl.program_id(0) * n_rows

    def start(j, carry):
        _row_copy(src_ref, dst_hbm, sem, j, dest_ref[base + j]).start()
        return carry

    lax.fori_loop(0, n_rows, start, 0, unroll=8)

    def wait(j, carry):
        _row_copy(src_ref, dst_hbm, sem, j, 0).wait()
        return carry

    lax.fori_loop(0, n_rows, wait, 0, unroll=8)


def _permute_rows(rows, dest):
    n, w = rows.shape
    r = _tile(n, 1024)
    grid_spec = pltpu.PrefetchScalarGridSpec(
        num_scalar_prefetch=1,
        grid=(n // r,),
        in_specs=[pl.BlockSpec((r, w), lambda i, dest_ref: (i, 0))],
        out_specs=pl.BlockSpec(memory_space=pl.ANY),
        scratch_shapes=[pltpu.SemaphoreType.DMA],
    )
    return pl.pallas_call(
        _permute_kernel,
        grid_spec=grid_spec,
        out_shape=jax.ShapeDtypeStruct((n, w), rows.dtype),
        compiler_params=_params(1),
    )(dest, rows)


def _combine_kernel(start_ref, nchunks_ref, first_ref, rows_hbm, lo_ref, hi_ref, x_ref, gt_ref, fg_ref,
                    o_ref, buf, sem, acc_lo, acc_hi, *, n_total):
    step = pl.program_id(0) * pl.num_programs(1) + pl.program_id(1)
    n_steps = pl.num_programs(0) * pl.num_programs(1)
    chunk = buf.shape[1]
    tm = acc_lo.shape[0]
    n_chunks = nchunks_ref[step]
    first = first_ref[step]

    def row0(s, c):
        return pl.multiple_of(jnp.minimum(start_ref[s] + c * chunk, n_total - chunk), SUBLANES)

    def fetch(s, c, slot):
        return pltpu.make_async_copy(rows_hbm.at[pl.ds(row0(s, c), chunk)], buf.at[slot], sem.at[slot])

    @pl.when(step == 0)
    def _():
        fetch(0, 0, 0).start()

    acc_lo[...] = jnp.zeros_like(acc_lo)
    acc_hi[...] = jnp.zeros_like(acc_hi)
    row_lo = lo_ref[0]
    row_hi = hi_ref[0]

    def consume(c, carry):
        slot = (first + c) % 2
        ends_step = c + 1 == n_chunks

        @pl.when(jnp.logical_not(ends_step))
        def _():
            fetch(step, c + 1, 1 - slot).start()

        @pl.when(jnp.logical_and(ends_step, step + 1 < n_steps))
        def _():
            fetch(step + 1, 0, 1 - slot).start()

        fetch(step, c, slot).wait()
        lo, hi = _unpack_bf16_pair(buf[slot])
        row = (row0(step, c) + lax.broadcasted_iota(jnp.int32, (tm, chunk), 1)).astype(F32)
        onehot = jnp.where(jnp.logical_and(row >= row_lo, row < row_hi), 1.0, 0.0).astype(BF16)
        acc_lo[...] += _dot(onehot, lo)
        acc_hi[...] += _dot(onehot, hi)
        return carry

    lax.fori_loop(0, n_chunks, consume, 0)
    y = jnp.concatenate([acc_lo[...], acc_hi[...]], axis=-1)
    x2 = x_ref[0] + gt_ref[0] * y
    ms = jnp.mean(x2 * x2, axis=-1, keepdims=True)
    o_ref[0] = x2 * lax.rsqrt(ms + RMS_EPS) * fg_ref[...]


def _combine(sorted_rows, row_lo, row_hi, x, mod, k_gate, final_g):
    b, l, d = x.shape
    n_total, half = sorted_rows.shape
    tm = _tile(l, 256)
    nt = l // tm
    chunk = COMBINE_CHUNK
    assert n_total >= chunk and n_total % SUBLANES == 0
    tile_lo = row_lo[:, ::tm].astype(jnp.int32).reshape(-1)
    tile_hi = row_hi[:, tm - 1::tm].astype(jnp.int32).reshape(-1)
    start = tile_lo // SUBLANES * SUBLANES
    n_chunks = jnp.maximum(1, (tile_hi - start + chunk - 1) // chunk)
    first = jnp.cumsum(n_chunks) - n_chunks
    col = lambda a: a.reshape(b, l, 1)
    grid_spec = pltpu.PrefetchScalarGridSpec(
        num_scalar_prefetch=3,
        grid=(b, nt),
        in_specs=[pl.BlockSpec(memory_space=pl.ANY),
                  pl.BlockSpec((1, tm, 1), lambda bi, i, *_: (bi, i, 0)),
                  pl.BlockSpec((1, tm, 1), lambda bi, i, *_: (bi, i, 0)),
                  pl.BlockSpec((1, tm, d), lambda bi, i, *_: (bi, i, 0)),
                  pl.BlockSpec((1, 1, d), lambda bi, i, *_: (bi, 0, k_gate)),
                  pl.BlockSpec((1, d), lambda bi, i, *_: (0, 0))],
        out_specs=pl.BlockSpec((1, tm, d), lambda bi, i, *_: (bi, i, 0)),
        scratch_shapes=[pltpu.VMEM((2, chunk, half), jnp.uint32), pltpu.SemaphoreType.DMA((2,)),
                        pltpu.VMEM((tm, half), F32), pltpu.VMEM((tm, half), F32)],
    )
    return pl.pallas_call(
        functools.partial(_combine_kernel, n_total=n_total),
        grid_spec=grid_spec,
        out_shape=jax.ShapeDtypeStruct((b, l, d), F32),
        compiler_params=_params(2),
    )(start, n_chunks.astype(jnp.int32), first.astype(jnp.int32), sorted_rows, col(row_lo), col(row_hi),
      x, mod, final_g.reshape(1, d))


def kernel(x, c, ctx, c_ctx, w_mod, b_mod, norm_mix_g, w_in, b_gate, w_fourier, na_rel_bias,
           w_na_out, w_out, norm_ffn_g, w_router, w1, w3, w2, final_norm_g):
    assert w_mod.shape[0] == 1, "single-layer stack only"
    b, l, d = x.shape
    ctx_len = ctx.shape[1]
    f_width = w_fourier.shape[1]
    na_width = w_na_out.shape[1]
    n_experts = w_router.shape[-1]
    in_width = w_in.shape[-1]
    off_q = f_width
    off_k = off_q + na_width
    off_g = off_k + 2 * na_width
    cap = EC_FACTOR * l // n_experts
    SH_M, SC_M, GT_M, SH_F, SC_F, GT_F = range(N_MOD)

    pad = (-(b + 1)) % 16
    cond = jnp.concatenate([c, c_ctx[None], jnp.zeros((pad, d), F32)], axis=0)
    mod_all = _ada_params(cond, w_mod[0], b_mod[0])
    mod = mod_all[:b].reshape(b, 1, N_MOD * d)
    mod_ctx = mod_all[b:b + 1].reshape(1, 1, N_MOD * d)

    w_in_bf = w_in[0].astype(BF16)
    xn = _norm_mod(x, norm_mix_g[0], mod, SH_M, SC_M)
    cn = _norm_mod(ctx, norm_mix_g[0], mod_ctx, SH_M, SC_M)
    b_full = jnp.concatenate([jnp.zeros((off_g,), F32), b_gate[0]]).reshape(1, in_width)
    z = _proj_in(xn.reshape(b * l, d), w_in_bf, b_full, off_g)
    zc = _matmul_cols(cn.reshape(b * ctx_len, d), w_in_bf, off_k, 2 * na_width)
    z3 = z.reshape(b, l, in_width)

    ab = _fourier_chan(z3, f_width)
    y_four = _fourier_pos(ab.reshape(b, 2 * l, f_width))
    bias_slabs = _na_bias_slabs(na_rel_bias[0], l // GRID_W)
    o_na = _neighbourhood_attention(z3, zc.reshape(b, ctx_len, 2 * na_width), bias_slabs, off_q, na_width)
    mixed = _merge(y_four.reshape(b * l, f_width), o_na.reshape(b * l, na_width),
                   w_fourier[0], w_na_out[0], z, off_g)
    x1 = _out_proj_residual(mixed, w_out[0], x.reshape(b * l, d), mod, GT_M, l)

    x1_3 = x1.reshape(b, l, d)
    aff_t = _router(x1_3, norm_ffn_g[0], mod, SH_F, SC_F, w_router[0].T)
    idx_t, dst_t, gate_t, cum = _select(aff_t.reshape(b * n_experts, l), cap, n_experts)

    def expert_major(slot_major, batch_stride):
        v = slot_major.T.reshape(b, n_experts, cap)
        offs = (jnp.arange(b, dtype=jnp.int32) * batch_stride)[None, :, None]
        return (jnp.transpose(v, (1, 0, 2)) + offs.astype(v.dtype)).reshape(-1)

    per_batch = n_experts * cap
    xin = _gather_norm(expert_major(idx_t, l), x1, norm_ffn_g[0], mod, SH_F, SC_F, cap, b)
    hid = _ffn_up(xin, w1[0], w3[0])
    out = _ffn_down(hid, w2[0], expert_major(gate_t, 0).reshape(-1, 1))
    sorted_rows = _permute_rows(out, expert_major(dst_t, per_batch))
    before = cum.reshape(b, n_experts, l)[:, 0, :] + (jnp.arange(b, dtype=F32) * per_batch)[:, None]
    after = jnp.concatenate([before[:, 1:], before[:, :1] + per_batch], axis=1)
    return _combine(sorted_rows, before, after, x1_3, mod, GT_F, final_norm_g)
```

```python
import functools

import numpy as np
import jax
import jax.numpy as jnp
from jax import lax
from jax.experimental import pallas as pl
from jax.experimental.pallas import tpu as pltpu

GRID_W = 64
F_GROUPS = 4
NA_HEAD_DIM = 128
NA_ROWS_MAX = 8
NA_COLS = 16
EC_FACTOR = 2
N_MOD = 6
RMS_EPS = 1e-6

LANES = 128
SUBLANES = 8
COMBINE_CHUNK = 256
VMEM_LIMIT_BYTES = 56 * 1024 * 1024
BISECT_STEPS = 160
NA_ROWS_PER_STEP = 8
NA_HEADS_PER_STEP = 4

F32 = jnp.float32
BF16 = jnp.bfloat16
NT_DIMS = (((1,), (1,)), ((), ()))


def _params(n_axes):
    return pltpu.CompilerParams(
        dimension_semantics=("arbitrary",) * n_axes, vmem_limit_bytes=VMEM_LIMIT_BYTES)


def _tile(dim, pref):
    t = min(dim, pref)
    while dim % t:
        t -= 1
    return t


def _dot(a, b):
    return jnp.dot(a, b, preferred_element_type=F32)


def _rms_modulate(x, gain, scale, shift):
    ms = jnp.mean(x * x, axis=-1, keepdims=True)
    y = x * lax.rsqrt(ms + RMS_EPS) * gain
    return y * (1.0 + scale) + shift


def _ada_kernel(c_ref, w_ref, b_ref, o_ref):
    cond = c_ref[...]
    act = cond * jax.nn.sigmoid(cond)
    o_ref[...] = _dot(act.astype(BF16), w_ref[...].astype(BF16)) + b_ref[...]


def _ada_params(cond, w_mod, b_mod):
    m, d = cond.shape
    n = w_mod.shape[1]
    tn = _tile(n, 512)
    return pl.pallas_call(
        _ada_kernel,
        grid=(n // tn,),
        in_specs=[pl.BlockSpec((m, d), lambda j: (0, 0)),
                  pl.BlockSpec((d, tn), lambda j: (0, j)),
                  pl.BlockSpec((1, tn), lambda j: (0, j))],
        out_specs=pl.BlockSpec((m, tn), lambda j: (0, j)),
        out_shape=jax.ShapeDtypeStruct((m, n), F32),
        compiler_params=_params(1),
    )(cond, w_mod, b_mod.reshape(1, n))


def _norm_mod_kernel(x_ref, g_ref, sc_ref, sh_ref, o_ref):
    o_ref[0] = _rms_modulate(x_ref[0], g_ref[...], sc_ref[0], sh_ref[0]).astype(o_ref.dtype)


def _norm_mod(x, gain, mod, k_shift, k_scale):
    b, l, d = x.shape
    per_batch = mod.shape[0] == b
    tm = _tile(l, 512)
    bsel = (lambda i: i) if per_batch else (lambda i: 0)
    return pl.pallas_call(
        _norm_mod_kernel,
        grid=(b, l // tm),
        in_specs=[pl.BlockSpec((1, tm, d), lambda bi, i: (bi, i, 0)),
                  pl.BlockSpec((1, d), lambda bi, i: (0, 0)),
                  pl.BlockSpec((1, 1, d), lambda bi, i: (bsel(bi), 0, k_scale)),
                  pl.BlockSpec((1, 1, d), lambda bi, i: (bsel(bi), 0, k_shift))],
        out_specs=pl.BlockSpec((1, tm, d), lambda bi, i: (bi, i, 0)),
        out_shape=jax.ShapeDtypeStruct((b, l, d), BF16),
        compiler_params=_params(2),
    )(x, gain.reshape(1, d), mod, mod)


def _proj_in_kernel(a_ref, w_ref, b_ref, o_ref, *, gate_block0):
    acc = _dot(a_ref[...], w_ref[...])
    j = pl.program_id(1)

    @pl.when(j < gate_block0)
    def _():
        o_ref[...] = acc.astype(o_ref.dtype)

    @pl.when(j >= gate_block0)
    def _():
        o_ref[...] = jax.nn.sigmoid(acc + b_ref[...]).astype(o_ref.dtype)


def _proj_in(a, w, b_full, off_gate):
    m, k = a.shape
    n = w.shape[1]
    tm = _tile(m, 1024)
    tn = _tile(np.gcd(n, off_gate), 1024)
    return pl.pallas_call(
        functools.partial(_proj_in_kernel, gate_block0=off_gate // tn),
        grid=(m // tm, n // tn),
        in_specs=[pl.BlockSpec((tm, k), lambda i, j: (i, 0)),
                  pl.BlockSpec((k, tn), lambda i, j: (0, j)),
                  pl.BlockSpec((1, tn), lambda i, j: (0, j))],
        out_specs=pl.BlockSpec((tm, tn), lambda i, j: (i, j)),
        out_shape=jax.ShapeDtypeStruct((m, n), BF16),
        compiler_params=_params(2),
    )(a, w, b_full)


def _mm_kernel(a_ref, w_ref, o_ref):
    o_ref[...] = _dot(a_ref[...], w_ref[...]).astype(o_ref.dtype)


def _matmul_cols(a, w, col0, ncols):
    m, k = a.shape
    tm = _tile(m, 1024)
    tn = _tile(np.gcd(ncols, col0) if col0 else ncols, 512)
    j0 = col0 // tn
    return pl.pallas_call(
        _mm_kernel,
        grid=(m // tm, ncols // tn),
        in_specs=[pl.BlockSpec((tm, k), lambda i, j: (i, 0)),
                  pl.BlockSpec((k, tn), lambda i, j: (0, j0 + j))],
        out_specs=pl.BlockSpec((tm, tn), lambda i, j: (i, j)),
        out_shape=jax.ShapeDtypeStruct((m, ncols), BF16),
        compiler_params=_params(2),
    )(a, w)


def _dft_tables(n, scale):
    k = np.arange(n)
    ang = 2.0 * np.pi * ((k[:, None] * k[None, :]) % n) / n
    return np.cos(ang) * scale, np.sin(ang) * scale


def _fourier_chan_kernel(u_ref, t_ref, o_ref):
    dg = u_ref.shape[-1]
    res = _dot(u_ref[0], t_ref[...])
    o_ref[0, 0] = res[:, :dg].astype(o_ref.dtype)
    o_ref[0, 1] = res[:, dg:].astype(o_ref.dtype)


def _fourier_chan(z3, f_width):
    b, l, _ = z3.shape
    dg = f_width // F_GROUPS
    cd, sd = _dft_tables(dg, dg ** -0.5)
    table = jnp.asarray(np.concatenate([cd, sd], axis=1), dtype=BF16)
    tm = _tile(l, 1024)
    return pl.pallas_call(
        _fourier_chan_kernel,
        grid=(b, l // tm, F_GROUPS),
        in_specs=[pl.BlockSpec((1, tm, dg), lambda bi, i, g: (bi, i, g)),
                  pl.BlockSpec((dg, 2 * dg), lambda bi, i, g: (0, 0))],
        out_specs=pl.BlockSpec((1, 2, tm, dg), lambda bi, i, g: (bi, 0, i, g)),
        out_shape=jax.ShapeDtypeStruct((b, 2, l, f_width), BF16),
        compiler_params=_params(3),
    )(z3, table)


def _fourier_pos_kernel(t_ref, ab_ref, o_ref):
    o_ref[0] = _dot(t_ref[...], ab_ref[0]).astype(o_ref.dtype)


def _fourier_pos(ab):
    b, l2, fw = ab.shape
    l = l2 // 2
    cl, sl = _dft_tables(l, l ** -0.5)
    table = jnp.asarray(np.concatenate([cl, -sl], axis=1), dtype=BF16)
    tm = _tile(l, 1024)
    tn = _tile(fw, 512)
    return pl.pallas_call(
        _fourier_pos_kernel,
        grid=(b, fw // tn, l // tm),
        in_specs=[pl.BlockSpec((tm, l2), lambda bi, j, i: (i, 0)),
                  pl.BlockSpec((1, l2, tn), lambda bi, j, i: (bi, 0, j))],
        out_specs=pl.BlockSpec((1, tm, tn), lambda bi, j, i: (bi, i, j)),
        out_shape=jax.ShapeDtypeStruct((b, l, fw), BF16),
        compiler_params=_params(3),
    )(table, ab)


def _na_geometry(rows):
    kr = min(NA_ROWS_MAX, rows)
    starts = [int(np.clip(r - kr // 2, 0, rows - kr)) - r + NA_ROWS_MAX - 1 for r in range(rows)]
    return kr, min(starts), max(starts) - min(starts) + 1


def _na_bias_slabs(rel_bias, rows):
    kr, min_start, n_slabs = _na_geometry(rows)
    cols = np.arange(GRID_W)
    win_start = np.clip(cols - NA_COLS // 2, 0, GRID_W - NA_COLS)
    rel_col = cols[None, :] - win_start[:, None]
    col_mask = (rel_col >= 0) & (rel_col < NA_COLS)
    dc_idx = np.clip(cols[None, :] - cols[:, None] + NA_COLS - 1, 0, 2 * NA_COLS - 2)
    neg = jnp.finfo(F32).min
    bias_c = jnp.where(col_mask[None, None], rel_bias.astype(F32)[:, :, dc_idx], neg)
    slabs = jnp.stack([bias_c[:, min_start + s:min_start + s + kr] for s in range(n_slabs)], axis=1)
    slabs = jnp.transpose(slabs, (0, 1, 3, 2, 4))
    h = rel_bias.shape[0]
    return slabs.reshape(h, n_slabs, GRID_W, kr * GRID_W)


def _na_kernel(q_ref, k_ref, v_ref, kc_ref, vc_ref, bias_ref, o_ref, s_ctx_all, p_ctx_all, o_loc_all,
               *, heads_per_step, **geometry):
    for hh in range(heads_per_step):
        cols = slice(hh * NA_HEAD_DIM, (hh + 1) * NA_HEAD_DIM)
        _na_one_head(q_ref.at[0, :, cols], k_ref.at[0, :, cols], v_ref.at[0, :, cols], kc_ref.at[0, :, cols],
                     vc_ref.at[0, :, cols], bias_ref.at[hh], o_ref.at[0, :, cols],
                     s_ctx_all, p_ctx_all, o_loc_all, **geometry)


def _na_one_head(q_ref, k_ref, v_ref, kc_ref, vc_ref, bias_ref, o_ref, s_ctx_all, p_ctx_all, o_loc_all,
                 *, rows, rows_per_step, kr, min_start, scale):
    s_ctx_all[...] = lax.dot_general(q_ref[...], kc_ref[...], NT_DIMS, preferred_element_type=F32) * scale

    def scores(r):
        rs = jnp.clip(r - kr // 2, 0, rows - kr)
        slab = rs - r + (NA_ROWS_MAX - 1 - min_start)
        q0 = pl.multiple_of(r * GRID_W, GRID_W)
        k0 = pl.multiple_of(rs * GRID_W, GRID_W)
        qr = q_ref[pl.ds(q0, GRID_W), :]
        kw = k_ref[pl.ds(k0, kr * GRID_W), :]
        s_loc = lax.dot_general(qr, kw, NT_DIMS, preferred_element_type=F32) * scale + bias_ref[slab]
        return q0, k0, s_loc

    def softmax(q0, s_loc):
        s_ctx = s_ctx_all[pl.ds(q0, GRID_W), :]
        m = jnp.maximum(jnp.max(s_loc, axis=-1, keepdims=True), jnp.max(s_ctx, axis=-1, keepdims=True))
        p_loc = jnp.exp(s_loc - m)
        p_ctx = jnp.exp(s_ctx - m)
        inv = 1.0 / (jnp.sum(p_loc, axis=-1, keepdims=True) + jnp.sum(p_ctx, axis=-1, keepdims=True))
        return (p_loc * inv).astype(BF16), (p_ctx * inv).astype(BF16)

    def row_group(g, carry):
        sc = [scores(g * rows_per_step + u) for u in range(rows_per_step)]
        pr = [softmax(q0, s_loc) for q0, _, s_loc in sc]
        for (q0, k0, _), (p_loc, p_ctx) in zip(sc, pr):
            p_ctx_all[pl.ds(q0, GRID_W), :] = p_ctx
            o_loc_all[pl.ds(q0, GRID_W), :] = _dot(p_loc, v_ref[pl.ds(k0, kr * GRID_W), :])
        return carry

    lax.fori_loop(0, rows // rows_per_step, row_group, 0)
    o_ref[...] = (o_loc_all[...] + _dot(p_ctx_all[...], vc_ref[...])).astype(o_ref.dtype)


def _neighbourhood_attention(z3, zc3, bias_slabs, off_q, na_width):
    b, l, _ = z3.shape
    ctx_len = zc3.shape[1]
    dh = NA_HEAD_DIM
    heads = na_width // dh
    rows = l // GRID_W
    kr, min_start, n_slabs = _na_geometry(rows)
    hp = _tile(heads, NA_HEADS_PER_STEP)
    w = hp * dh
    qb, kb, vb = off_q // w, (off_q + na_width) // w, (off_q + 2 * na_width) // w
    kern = functools.partial(_na_kernel, heads_per_step=hp, rows=rows,
                             rows_per_step=_tile(rows, NA_ROWS_PER_STEP), kr=kr,
                             min_start=min_start, scale=dh ** -0.5)
    return pl.pallas_call(
        kern,
        grid=(heads // hp, b),
        in_specs=[pl.BlockSpec((1, l, w), lambda h, bi: (bi, 0, qb + h)),
                  pl.BlockSpec((1, l, w), lambda h, bi: (bi, 0, kb + h)),
                  pl.BlockSpec((1, l, w), lambda h, bi: (bi, 0, vb + h)),
                  pl.BlockSpec((1, ctx_len, w), lambda h, bi: (bi, 0, h)),
                  pl.BlockSpec((1, ctx_len, w), lambda h, bi: (bi, 0, heads // hp + h)),
                  pl.BlockSpec((hp, n_slabs, GRID_W, kr * GRID_W), lambda h, bi: (h, 0, 0, 0))],
        out_specs=pl.BlockSpec((1, l, w), lambda h, bi: (bi, 0, h)),
        out_shape=jax.ShapeDtypeStruct((b, l, na_width), BF16),
        scratch_shapes=[pltpu.VMEM((l, ctx_len), F32), pltpu.VMEM((l, ctx_len), BF16),
                        pltpu.VMEM((l, dh), F32)],
        compiler_params=_params(2),
    )(z3, z3, z3, zc3, zc3, bias_slabs)


def _merge_kernel(yf_ref, on_ref, wf_ref, wn_ref, gf_ref, gn_ref, o_ref):
    y_f = _dot(yf_ref[...], wf_ref[...].astype(BF16))
    y_n = _dot(on_ref[...], wn_ref[...].astype(BF16))
    o_ref[...] = (gf_ref[...].astype(F32) * y_f + gn_ref[...].astype(F32) * y_n).astype(o_ref.dtype)


def _merge(y_four, o_na, w_f, w_n, z, off_gate):
    m, kf = y_four.shape
    kn = o_na.shape[1]
    d = w_f.shape[1]
    tm = _tile(m, 1024)
    tn = _tile(np.gcd(d, off_gate), 512)
    gf0, gn0 = off_gate // tn, (off_gate + d) // tn
    return pl.pallas_call(
        _merge_kernel,
        grid=(m // tm, d // tn),
        in_specs=[pl.BlockSpec((tm, kf), lambda i, j: (i, 0)),
                  pl.BlockSpec((tm, kn), lambda i, j: (i, 0)),
                  pl.BlockSpec((kf, tn), lambda i, j: (0, j)),
                  pl.BlockSpec((kn, tn), lambda i, j: (0, j)),
                  pl.BlockSpec((tm, tn), lambda i, j: (i, gf0 + j)),
                  pl.BlockSpec((tm, tn), lambda i, j: (i, gn0 + j))],
        out_specs=pl.BlockSpec((tm, tn), lambda i, j: (i, j)),
        out_shape=jax.ShapeDtypeStruct((m, d), BF16),
        compiler_params=_params(2),
    )(y_four, o_na, w_f, w_n, z, z)


def _out_proj_kernel(a_ref, w_ref, x_ref, gt_ref, o_ref):
    o_ref[...] = x_ref[...] + gt_ref[0] * _dot(a_ref[...], w_ref[...].astype(BF16))


def _out_proj_residual(a, w, x2d, mod, k_gate, seq):
    m, k = a.shape
    d = w.shape[1]
    tm = _tile(seq, 1024)
    tn = _tile(d, 512)
    gate0 = k_gate * (d // tn)
    return pl.pallas_call(
        _out_proj_kernel,
        grid=(m // tm, d // tn),
        in_specs=[pl.BlockSpec((tm, k), lambda i, j: (i, 0)),
                  pl.BlockSpec((k, tn), lambda i, j: (0, j)),
                  pl.BlockSpec((tm, tn), lambda i, j: (i, j)),
                  pl.BlockSpec((1, 1, tn), lambda i, j: (i * tm // seq, 0, gate0 + j))],
        out_specs=pl.BlockSpec((tm, tn), lambda i, j: (i, j)),
        out_shape=jax.ShapeDtypeStruct((m, d), F32),
        compiler_params=_params(2),
    )(a, w, x2d, mod)


def _router_kernel(x_ref, g_ref, sc_ref, sh_ref, wr_ref, aff_ref):
    xn = _rms_modulate(x_ref[0], g_ref[...], sc_ref[0], sh_ref[0])
    x_hi = xn.astype(BF16)
    x_lo = (xn - x_hi.astype(F32)).astype(BF16)
    w = wr_ref[...]
    w_hi = w.astype(BF16)
    w_lo = (w - w_hi.astype(F32)).astype(BF16)
    logits = (lax.dot_general(w_hi, x_hi, NT_DIMS, preferred_element_type=F32)
              + lax.dot_general(w_lo, x_hi, NT_DIMS, preferred_element_type=F32)
              + lax.dot_general(w_hi, x_lo, NT_DIMS, preferred_element_type=F32))
    ex = jnp.exp(logits - jnp.max(logits, axis=0, keepdims=True))
    aff_ref[0] = ex / jnp.sum(ex, axis=0, keepdims=True)


def _router(x, gain, mod, k_shift, k_scale, w_router_t):
    b, l, d = x.shape
    e = w_router_t.shape[0]
    tm = _tile(l, 512)
    return pl.pallas_call(
        _router_kernel,
        grid=(b, l // tm),
        in_specs=[pl.BlockSpec((1, tm, d), lambda bi, i: (bi, i, 0)),
                  pl.BlockSpec((1, d), lambda bi, i: (0, 0)),
                  pl.BlockSpec((1, 1, d), lambda bi, i: (bi, 0, k_scale)),
                  pl.BlockSpec((1, 1, d), lambda bi, i: (bi, 0, k_shift)),
                  pl.BlockSpec((e, d), lambda bi, i: (0, 0))],
        out_specs=pl.BlockSpec((1, e, tm), lambda bi, i: (bi, 0, i)),
        out_shape=jax.ShapeDtypeStruct((b, e, l), F32),
        compiler_params=_params(2),
    )(x, gain.reshape(1, d), mod, mod, w_router_t)


def _select_kernel(a_ref, tri_ref, same_ref, lower_ref, idx_ref, dst_ref, gate_ref, cum_ref, pos_ref, rank_ref,
                   *, cap):
    a = a_ref[...]
    n_rows, l = a.shape
    capf = float(cap)

    def count(mask):
        return jnp.sum(jnp.where(mask, 1.0, 0.0), axis=-1, keepdims=True)

    def halve(_, lo_hi):
        lo, hi = lo_hi
        mid = 0.5 * lo + 0.5 * hi
        ok = count(a_ref[...] >= mid) >= capf
        return jnp.where(ok, mid, lo), jnp.where(ok, hi, mid)

    _, hi = lax.fori_loop(0, BISECT_STEPS, halve,
                          (jnp.zeros((n_rows, 1), F32), jnp.full((n_rows, 1), 2.0, F32)))
    tau = jnp.max(jnp.where(a < hi, a, -1.0), axis=-1, keepdims=True)
    above = a > tau
    tied = a == tau
    need = capf - count(above)
    tri = tri_ref[...]
    tied_before = _dot(jnp.where(tied, 1.0, 0.0).astype(BF16), tri)
    chosen = above | (tied & (tied_before < need))
    chosen_b = jnp.where(chosen, 1.0, 0.0).astype(BF16)
    pos = _dot(chosen_b, tri)
    pos_ref[...] = jnp.where(chosen, pos, -1.0)
    per_token = _dot(same_ref[...], chosen_b)
    before_token = _dot(per_token.astype(BF16), tri)
    cum_ref[...] = before_token
    rank_ref[...] = before_token + _dot(lower_ref[...], chosen_b)

    tok = lax.broadcasted_iota(jnp.int32, (cap, l), 1).astype(F32)
    slot = lax.broadcasted_iota(jnp.int32, (cap, l), 0).astype(F32)
    lane = lax.broadcasted_iota(jnp.int32, (cap, n_rows), 1)

    def compact(r, accs):
        hit = pos_ref[pl.ds(r, 1), :] == slot
        picked = [jnp.sum(jnp.where(hit, v, 0.0), axis=-1, keepdims=True)
                  for v in (tok, rank_ref[pl.ds(r, 1), :], a_ref[pl.ds(r, 1), :])]
        return tuple(jnp.where(lane == r, p, acc) for p, acc in zip(picked, accs))

    zeros = jnp.zeros((cap, n_rows), F32)
    ids, ranks, gates = lax.fori_loop(0, n_rows, compact, (zeros, zeros, zeros))
    idx_ref[...] = ids.astype(jnp.int32)
    dst_ref[...] = ranks.astype(jnp.int32)
    gate_ref[...] = gates


def _select(aff_rows, cap, n_experts):
    n_rows, l = aff_rows.shape
    tri = jnp.asarray(np.triu(np.ones((l, l), np.float32), k=1), dtype=BF16)
    group = np.arange(n_rows) // n_experts
    same = group[:, None] == group[None, :]
    lower = same & (np.arange(n_rows)[None, :] < np.arange(n_rows)[:, None])
    full = lambda shape: pl.BlockSpec(shape, lambda i: (0, 0))
    slot_major = jax.ShapeDtypeStruct((cap, n_rows), jnp.int32)
    return pl.pallas_call(
        functools.partial(_select_kernel, cap=cap),
        grid=(1,),
        in_specs=[full((n_rows, l)), full((l, l)), full((n_rows, n_rows)), full((n_rows, n_rows))],
        out_specs=[full((cap, n_rows)), full((cap, n_rows)), full((cap, n_rows)), full((n_rows, l))],
        out_shape=[slot_major, slot_major, jax.ShapeDtypeStruct((cap, n_rows), F32),
                   jax.ShapeDtypeStruct((n_rows, l), F32)],
        scratch_shapes=[pltpu.VMEM((n_rows, l), F32), pltpu.VMEM((n_rows, l), F32)],
        compiler_params=_params(1),
    )(aff_rows, tri, jnp.asarray(same, dtype=BF16), jnp.asarray(lower, dtype=BF16))


def _gather_copy(x_hbm, buf, sem, row, j):
    return pltpu.make_async_copy(x_hbm.at[pl.ds(row, 1)], buf.at[pl.ds(j, 1)], sem)


def _gather_norm_kernel(rows_ref, x_hbm, g_ref, sc_ref, sh_ref, o_ref, buf, sem, *, n_rows):
    i = pl.program_id(0)

    def start_rows(step, slot):
        def start(j, carry):
            _gather_copy(x_hbm, buf.at[slot], sem.at[slot], rows_ref[step * n_rows + j], j).start()
            return carry

        lax.fori_loop(0, n_rows, start, 0, unroll=8)

    @pl.when(i == 0)
    def _():
        start_rows(0, 0)

    @pl.when(i + 1 < pl.num_programs(0))
    def _():
        start_rows(i + 1, (i + 1) % 2)

    slot = i % 2
    pltpu.make_async_copy(x_hbm.at[pl.ds(0, n_rows)], buf.at[slot], sem.at[slot]).wait()
    o_ref[...] = _rms_modulate(buf[slot], g_ref[...], sc_ref[0], sh_ref[0]).astype(o_ref.dtype)


def _gather_norm(flat_rows, x2d, gain, mod, k_shift, k_scale, cap, n_batch):
    n = flat_rows.shape[0]
    d = x2d.shape[1]
    r = _tile(cap, 256)
    per_b = cap // r
    bsel = lambda i, rows: (i // per_b) % n_batch
    grid_spec = pltpu.PrefetchScalarGridSpec(
        num_scalar_prefetch=1,
        grid=(n // r,),
        in_specs=[pl.BlockSpec(memory_space=pl.ANY),
                  pl.BlockSpec((1, d), lambda i, rows: (0, 0)),
                  pl.BlockSpec((1, 1, d), lambda i, rows: (bsel(i, rows), 0, k_scale)),
                  pl.BlockSpec((1, 1, d), lambda i, rows: (bsel(i, rows), 0, k_shift))],
        out_specs=pl.BlockSpec((r, d), lambda i, rows: (i, 0)),
        scratch_shapes=[pltpu.VMEM((2, r, d), F32), pltpu.SemaphoreType.DMA((2,))],
    )
    return pl.pallas_call(
        functools.partial(_gather_norm_kernel, n_rows=r),
        grid_spec=grid_spec,
        out_shape=jax.ShapeDtypeStruct((n, d), BF16),
        compiler_params=_params(1),
    )(flat_rows, x2d, gain.reshape(1, d), mod, mod)


def _ffn_up_kernel(x_ref, w1_ref, w3_ref, o_ref):
    xin = x_ref[...]
    a = _dot(xin, w1_ref[0].astype(BF16))
    o_ref[...] = (a * jax.nn.sigmoid(a) * _dot(xin, w3_ref[0].astype(BF16))).astype(o_ref.dtype)


def _ffn_up(xin, w1, w3):
    n, d = xin.shape
    e, _, ff = w1.shape
    rows_e = n // e
    tm = _tile(rows_e, 1024)
    tn = _tile(ff, 256)
    per_e = rows_e // tm
    return pl.pallas_call(
        _ffn_up_kernel,
        grid=(e, per_e, ff // tn),
        in_specs=[pl.BlockSpec((tm, d), lambda ei, i, j: (ei * per_e + i, 0)),
                  pl.BlockSpec((1, d, tn), lambda ei, i, j: (ei, 0, j)),
                  pl.BlockSpec((1, d, tn), lambda ei, i, j: (ei, 0, j))],
        out_specs=pl.BlockSpec((tm, tn), lambda ei, i, j: (ei * per_e + i, j)),
        out_shape=jax.ShapeDtypeStruct((n, ff), BF16),
        compiler_params=_params(3),
    )(xin, w1, w3)


def _bf16_bits(x):
    return lax.bitcast_convert_type(x.astype(BF16).astype(F32), jnp.uint32)


def _pack_bf16_pair(lo, hi):
    return (_bf16_bits(lo) >> 16) | _bf16_bits(hi)


def _unpack_bf16_pair(words):
    lo = lax.bitcast_convert_type(words << 16, F32).astype(BF16)
    hi = lax.bitcast_convert_type(words & jnp.uint32(0xFFFF0000), F32).astype(BF16)
    return lo, hi


def _ffn_down_kernel(h_ref, wa_ref, wb_ref, g_ref, o_ref):
    h = h_ref[...]
    gate = g_ref[...]
    o_ref[...] = _pack_bf16_pair(_dot(h, wa_ref[0].astype(BF16)) * gate,
                                 _dot(h, wb_ref[0].astype(BF16)) * gate)


def _ffn_down(hid, w2, gate_col):
    n, ff = hid.shape
    e, _, d = w2.shape
    rows_e = n // e
    half = d // 2
    tw = _tile(half, 256)
    nb = half // tw
    return pl.pallas_call(
        _ffn_down_kernel,
        grid=(e, nb),
        in_specs=[pl.BlockSpec((rows_e, ff), lambda ei, j: (ei, 0)),
                  pl.BlockSpec((1, ff, tw), lambda ei, j: (ei, 0, j)),
                  pl.BlockSpec((1, ff, tw), lambda ei, j: (ei, 0, nb + j)),
                  pl.BlockSpec((rows_e, 1), lambda ei, j: (ei, 0))],
        out_specs=pl.BlockSpec((rows_e, tw), lambda ei, j: (ei, j)),
        out_shape=jax.ShapeDtypeStruct((n, half), jnp.uint32),
        compiler_params=_params(2),
    )(hid, w2, w2, gate_col)


def _row_copy(src, dst_hbm, sem, j, row):
    return pltpu.make_async_copy(src.at[pl.ds(j, 1)], dst_hbm.at[pl.ds(row, 1)], sem)


def _permute_kernel(dest_ref, src_ref, dst_hbm, sem):
    n_rows = src_ref.shape[0]
    base = pl.program_id(0) * n_rows

    def start(j, carry):
        _row_copy(src_ref, dst_hbm, sem, j, dest_ref[base + j]).start()
        return carry

    lax.fori_loop(0, n_rows, start, 0, unroll=8)
    pltpu.make_async_copy(src_ref, dst_hbm.at[pl.ds(0, n_rows)], sem).wait()


def _permute_rows(rows, dest):
    n, w = rows.shape
    r = _tile(n, 1024)
    grid_spec = pltpu.PrefetchScalarGridSpec(
        num_scalar_prefetch=1,
        grid=(n // r,),
        in_specs=[pl.BlockSpec((r, w), lambda i, dest_ref: (i, 0))],
        out_specs=pl.BlockSpec(memory_space=pl.ANY),
        scratch_shapes=[pltpu.SemaphoreType.DMA],
    )
    return pl.pallas_call(
        _permute_kernel,
        grid_spec=grid_spec,
        out_shape=jax.ShapeDtypeStruct((n, w), rows.dtype),
        compiler_params=_params(1),
    )(dest, rows)


def _combine_kernel(start_ref, nchunks_ref, first_ref, rows_hbm, lo_ref, hi_ref, x_ref, gt_ref, fg_ref,
                    o_ref, buf, sem, acc_lo, acc_hi, *, n_total):
    step = pl.program_id(0) * pl.num_programs(1) + pl.program_id(1)
    n_steps = pl.num_programs(0) * pl.num_programs(1)
    chunk = buf.shape[1]
    tm = acc_lo.shape[0]
    n_chunks = nchunks_ref[step]
    first = first_ref[step]

    def row0(s, c):
        return pl.multiple_of(jnp.minimum(start_ref[s] + c * chunk, n_total - chunk), SUBLANES)

    def fetch(s, c, slot):
        return pltpu.make_async_copy(rows_hbm.at[pl.ds(row0(s, c), chunk)], buf.at[slot], sem.at[slot])

    @pl.when(step == 0)
    def _():
        fetch(0, 0, 0).start()

    acc_lo[...] = jnp.zeros_like(acc_lo)
    acc_hi[...] = jnp.zeros_like(acc_hi)
    row_lo = lo_ref[0]
    row_hi = hi_ref[0]

    def consume(c, carry):
        slot = (first + c) % 2
        ends_step = c + 1 == n_chunks

        @pl.when(jnp.logical_not(ends_step))
        def _():
            fetch(step, c + 1, 1 - slot).start()

        @pl.when(jnp.logical_and(ends_step, step + 1 < n_steps))
        def _():
            fetch(step + 1, 0, 1 - slot).start()

        fetch(step, c, slot).wait()
        lo, hi = _unpack_bf16_pair(buf[slot])
        row = (row0(step, c) + lax.broadcasted_iota(jnp.int32, (tm, chunk), 1)).astype(F32)
        first_row = jnp.maximum(row_lo, (start_ref[step] + c * chunk).astype(F32))
        onehot = jnp.where(jnp.logical_and(row >= first_row, row < row_hi), 1.0, 0.0).astype(BF16)
        acc_lo[...] += _dot(onehot, lo)
        acc_hi[...] += _dot(onehot, hi)
        return carry

    lax.fori_loop(0, n_chunks, consume, 0)
    y = jnp.concatenate([acc_lo[...], acc_hi[...]], axis=-1)
    x2 = x_ref[0] + gt_ref[0] * y
    ms = jnp.mean(x2 * x2, axis=-1, keepdims=True)
    o_ref[0] = x2 * lax.rsqrt(ms + RMS_EPS) * fg_ref[...]


def _combine(sorted_rows, row_lo, row_hi, x, mod, k_gate, final_g):
    b, l, d = x.shape
    n_total, half = sorted_rows.shape
    tm = _tile(l, 256)
    nt = l // tm
    chunk = COMBINE_CHUNK
    assert n_total >= chunk and n_total % SUBLANES == 0
    tile_lo = row_lo[:, ::tm].astype(jnp.int32).reshape(-1)
    tile_hi = row_hi[:, tm - 1::tm].astype(jnp.int32).reshape(-1)
    start = tile_lo // SUBLANES * SUBLANES
    n_chunks = jnp.maximum(1, (tile_hi - start + chunk - 1) // chunk)
    first = jnp.cumsum(n_chunks) - n_chunks
    col = lambda a: a.reshape(b, l, 1)
    grid_spec = pltpu.PrefetchScalarGridSpec(
        num_scalar_prefetch=3,
        grid=(b, nt),
        in_specs=[pl.BlockSpec(memory_space=pl.ANY),
                  pl.BlockSpec((1, tm, 1), lambda bi, i, *_: (bi, i, 0)),
                  pl.BlockSpec((1, tm, 1), lambda bi, i, *_: (bi, i, 0)),
                  pl.BlockSpec((1, tm, d), lambda bi, i, *_: (bi, i, 0)),
                  pl.BlockSpec((1, 1, d), lambda bi, i, *_: (bi, 0, k_gate)),
                  pl.BlockSpec((1, d), lambda bi, i, *_: (0, 0))],
        out_specs=pl.BlockSpec((1, tm, d), lambda bi, i, *_: (bi, i, 0)),
        scratch_shapes=[pltpu.VMEM((2, chunk, half), jnp.uint32), pltpu.SemaphoreType.DMA((2,)),
                        pltpu.VMEM((tm, half), F32), pltpu.VMEM((tm, half), F32)],
    )
    return pl.pallas_call(
        functools.partial(_combine_kernel, n_total=n_total),
        grid_spec=grid_spec,
        out_shape=jax.ShapeDtypeStruct((b, l, d), F32),
        compiler_params=_params(2),
    )(start, n_chunks.astype(jnp.int32), first.astype(jnp.int32), sorted_rows, col(row_lo), col(row_hi),
      x, mod, final_g.reshape(1, d))


def kernel(x, c, ctx, c_ctx, w_mod, b_mod, norm_mix_g, w_in, b_gate, w_fourier, na_rel_bias,
           w_na_out, w_out, norm_ffn_g, w_router, w1, w3, w2, final_norm_g):
    assert w_mod.shape[0] == 1, "single-layer stack only"
    b, l, d = x.shape
    ctx_len = ctx.shape[1]
    f_width = w_fourier.shape[1]
    na_width = w_na_out.shape[1]
    n_experts = w_router.shape[-1]
    in_width = w_in.shape[-1]
    off_q = f_width
    off_k = off_q + na_width
    off_g = off_k + 2 * na_width
    cap = EC_FACTOR * l // n_experts
    SH_M, SC_M, GT_M, SH_F, SC_F, GT_F = range(N_MOD)

    pad = (-(b + 1)) % 16
    cond = jnp.concatenate([c, c_ctx[None], jnp.zeros((pad, d), F32)], axis=0)
    mod_all = _ada_params(cond, w_mod[0], b_mod[0])
    mod = mod_all[:b].reshape(b, 1, N_MOD * d)
    mod_ctx = mod_all[b:b + 1].reshape(1, 1, N_MOD * d)

    w_in_bf = w_in[0].astype(BF16)
    xn = _norm_mod(x, norm_mix_g[0], mod, SH_M, SC_M)
    cn = _norm_mod(ctx, norm_mix_g[0], mod_ctx, SH_M, SC_M)
    b_full = jnp.concatenate([jnp.zeros((off_g,), F32), b_gate[0]]).reshape(1, in_width)
    z = _proj_in(xn.reshape(b * l, d), w_in_bf, b_full, off_g)
    zc = _matmul_cols(cn.reshape(b * ctx_len, d), w_in_bf, off_k, 2 * na_width)
    z3 = z.reshape(b, l, in_width)

    ab = _fourier_chan(z3, f_width)
    y_four = _fourier_pos(ab.reshape(b, 2 * l, f_width))
    bias_slabs = _na_bias_slabs(na_rel_bias[0], l // GRID_W)
    o_na = _neighbourhood_attention(z3, zc.reshape(b, ctx_len, 2 * na_width), bias_slabs, off_q, na_width)
    mixed = _merge(y_four.reshape(b * l, f_width), o_na.reshape(b * l, na_width),
                   w_fourier[0], w_na_out[0], z, off_g)
    x1 = _out_proj_residual(mixed, w_out[0], x.reshape(b * l, d), mod, GT_M, l)

    x1_3 = x1.reshape(b, l, d)
    aff_t = _router(x1_3, norm_ffn_g[0], mod, SH_F, SC_F, w_router[0].T)
    idx_t, dst_t, gate_t, cum = _select(aff_t.reshape(b * n_experts, l), cap, n_experts)

    def expert_major(slot_major, batch_stride):
        v = slot_major.T.reshape(b, n_experts, cap)
        offs = (jnp.arange(b, dtype=jnp.int32) * batch_stride)[None, :, None]
        return (jnp.transpose(v, (1, 0, 2)) + offs.astype(v.dtype)).reshape(-1)

    per_batch = n_experts * cap
    xin = _gather_norm(expert_major(idx_t, l), x1, norm_ffn_g[0], mod, SH_F, SC_F, cap, b)
    hid = _ffn_up(xin, w1[0], w3[0])
    out = _ffn_down(hid, w2[0], expert_major(gate_t, 0).reshape(-1, 1))
    sorted_rows = _permute_rows(out, expert_major(dst_t, per_batch))
    before = cum.reshape(b, n_experts, l)[:, 0, :] + (jnp.arange(b, dtype=F32) * per_batch)[:, None]
    after = jnp.concatenate([before[:, 1:], before[:, :1] + per_batch], axis=1)
    return _combine(sorted_rows, before, after, x1_3, mod, GT_F, final_norm_g)
```

```python
import functools

import numpy as np
import jax
import jax.numpy as jnp
from jax import lax
from jax.experimental import pallas as pl
from jax.experimental.pallas import tpu as pltpu

GRID_W = 64
F_GROUPS = 4
NA_HEAD_DIM = 128
NA_ROWS_MAX = 8
NA_COLS = 16
EC_FACTOR = 2
N_MOD = 6
RMS_EPS = 1e-6

SUBLANES = 8
COMBINE_CHUNK = 256
VMEM_LIMIT_BYTES = 56 * 1024 * 1024
BISECT_STEPS = 160
NA_ROWS_PER_STEP = 16
NA_HEADS_PER_STEP = 4

F32 = jnp.float32
BF16 = jnp.bfloat16
NT_DIMS = (((1,), (1,)), ((), ()))


def _params(n_axes):
    return pltpu.CompilerParams(
        dimension_semantics=("arbitrary",) * n_axes, vmem_limit_bytes=VMEM_LIMIT_BYTES)


def _tile(dim, pref):
    t = min(dim, pref)
    while dim % t:
        t -= 1
    return t


def _dot(a, b):
    return jnp.dot(a, b, preferred_element_type=F32)


def _rms_modulate(x, gain, scale, shift):
    ms = jnp.mean(x * x, axis=-1, keepdims=True)
    y = x * lax.rsqrt(ms + RMS_EPS) * gain
    return y * (1.0 + scale) + shift


def _ada_kernel(c_ref, w_ref, b_ref, o_ref):
    cond = c_ref[...]
    act = cond * jax.nn.sigmoid(cond)
    o_ref[...] = _dot(act.astype(BF16), w_ref[...].astype(BF16)) + b_ref[...]


def _ada_params(cond, w_mod, b_mod):
    m, d = cond.shape
    n = w_mod.shape[1]
    tn = _tile(n, 512)
    return pl.pallas_call(
        _ada_kernel,
        grid=(n // tn,),
        in_specs=[pl.BlockSpec((m, d), lambda j: (0, 0)),
                  pl.BlockSpec((d, tn), lambda j: (0, j)),
                  pl.BlockSpec((1, tn), lambda j: (0, j))],
        out_specs=pl.BlockSpec((m, tn), lambda j: (0, j)),
        out_shape=jax.ShapeDtypeStruct((m, n), F32),
        compiler_params=_params(1),
    )(cond, w_mod, b_mod.reshape(1, n))


def _norm_mod_kernel(x_ref, g_ref, sc_ref, sh_ref, o_ref):
    o_ref[0] = _rms_modulate(x_ref[0], g_ref[...], sc_ref[0], sh_ref[0]).astype(o_ref.dtype)


def _norm_mod(x, gain, mod, k_shift, k_scale):
    b, l, d = x.shape
    per_batch = mod.shape[0] == b
    tm = _tile(l, 512)
    bsel = (lambda i: i) if per_batch else (lambda i: 0)
    return pl.pallas_call(
        _norm_mod_kernel,
        grid=(b, l // tm),
        in_specs=[pl.BlockSpec((1, tm, d), lambda bi, i: (bi, i, 0)),
                  pl.BlockSpec((1, d), lambda bi, i: (0, 0)),
                  pl.BlockSpec((1, 1, d), lambda bi, i: (bsel(bi), 0, k_scale)),
                  pl.BlockSpec((1, 1, d), lambda bi, i: (bsel(bi), 0, k_shift))],
        out_specs=pl.BlockSpec((1, tm, d), lambda bi, i: (bi, i, 0)),
        out_shape=jax.ShapeDtypeStruct((b, l, d), BF16),
        compiler_params=_params(2),
    )(x, gain.reshape(1, d), mod, mod)


def _proj_in_kernel(a_ref, w_ref, b_ref, o_ref, *, gate_block0):
    acc = _dot(a_ref[...], w_ref[...])
    j = pl.program_id(1)

    @pl.when(j < gate_block0)
    def _():
        o_ref[...] = acc.astype(o_ref.dtype)

    @pl.when(j >= gate_block0)
    def _():
        o_ref[...] = jax.nn.sigmoid(acc + b_ref[...]).astype(o_ref.dtype)


def _proj_in(a, w, b_full, off_gate):
    m, k = a.shape
    n = w.shape[1]
    tm = _tile(m, 1024)
    tn = _tile(np.gcd(n, off_gate), 1024)
    return pl.pallas_call(
        functools.partial(_proj_in_kernel, gate_block0=off_gate // tn),
        grid=(m // tm, n // tn),
        in_specs=[pl.BlockSpec((tm, k), lambda i, j: (i, 0)),
                  pl.BlockSpec((k, tn), lambda i, j: (0, j)),
                  pl.BlockSpec((1, tn), lambda i, j: (0, j))],
        out_specs=pl.BlockSpec((tm, tn), lambda i, j: (i, j)),
        out_shape=jax.ShapeDtypeStruct((m, n), BF16),
        compiler_params=_params(2),
    )(a, w, b_full)


def _mm_kernel(a_ref, w_ref, o_ref):
    o_ref[...] = _dot(a_ref[...], w_ref[...]).astype(o_ref.dtype)


def _matmul_cols(a, w, col0, ncols):
    m, k = a.shape
    tm = _tile(m, 1024)
    tn = _tile(np.gcd(ncols, col0) if col0 else ncols, 512)
    j0 = col0 // tn
    return pl.pallas_call(
        _mm_kernel,
        grid=(m // tm, ncols // tn),
        in_specs=[pl.BlockSpec((tm, k), lambda i, j: (i, 0)),
                  pl.BlockSpec((k, tn), lambda i, j: (0, j0 + j))],
        out_specs=pl.BlockSpec((tm, tn), lambda i, j: (i, j)),
        out_shape=jax.ShapeDtypeStruct((m, ncols), BF16),
        compiler_params=_params(2),
    )(a, w)


def _dft_tables(n, scale):
    k = np.arange(n)
    ang = 2.0 * np.pi * ((k[:, None] * k[None, :]) % n) / n
    return np.cos(ang) * scale, np.sin(ang) * scale


def _fourier_chan_kernel(u_ref, t_ref, o_ref):
    dg = u_ref.shape[-1]
    res = _dot(u_ref[0], t_ref[...])
    o_ref[0, 0] = res[:, :dg].astype(o_ref.dtype)
    o_ref[0, 1] = res[:, dg:].astype(o_ref.dtype)


def _fourier_chan(z3, f_width):
    b, l, _ = z3.shape
    dg = f_width // F_GROUPS
    cd, sd = _dft_tables(dg, dg ** -0.5)
    table = jnp.asarray(np.concatenate([cd, sd], axis=1), dtype=BF16)
    tm = _tile(l, 1024)
    return pl.pallas_call(
        _fourier_chan_kernel,
        grid=(b, l // tm, F_GROUPS),
        in_specs=[pl.BlockSpec((1, tm, dg), lambda bi, i, g: (bi, i, g)),
                  pl.BlockSpec((dg, 2 * dg), lambda bi, i, g: (0, 0))],
        out_specs=pl.BlockSpec((1, 2, tm, dg), lambda bi, i, g: (bi, 0, i, g)),
        out_shape=jax.ShapeDtypeStruct((b, 2, l, f_width), BF16),
        compiler_params=_params(3),
    )(z3, table)


def _fourier_pos_kernel(t_ref, ab_ref, o_ref):
    o_ref[0] = _dot(t_ref[...], ab_ref[0]).astype(o_ref.dtype)


def _fourier_pos(ab):
    b, l2, fw = ab.shape
    l = l2 // 2
    cl, sl = _dft_tables(l, l ** -0.5)
    table = jnp.asarray(np.concatenate([cl, -sl], axis=1), dtype=BF16)
    tm = _tile(l, 1024)
    tn = _tile(fw, 512)
    return pl.pallas_call(
        _fourier_pos_kernel,
        grid=(b, fw // tn, l // tm),
        in_specs=[pl.BlockSpec((tm, l2), lambda bi, j, i: (i, 0)),
                  pl.BlockSpec((1, l2, tn), lambda bi, j, i: (bi, 0, j))],
        out_specs=pl.BlockSpec((1, tm, tn), lambda bi, j, i: (bi, i, j)),
        out_shape=jax.ShapeDtypeStruct((b, l, fw), BF16),
        compiler_params=_params(3),
    )(table, ab)


def _na_geometry(rows):
    kr = min(NA_ROWS_MAX, rows)
    starts = [int(np.clip(r - kr // 2, 0, rows - kr)) - r + NA_ROWS_MAX - 1 for r in range(rows)]
    return kr, min(starts), max(starts) - min(starts) + 1


def _na_bias_slabs(rel_bias, rows):
    kr, min_start, n_slabs = _na_geometry(rows)
    cols = np.arange(GRID_W)
    win_start = np.clip(cols - NA_COLS // 2, 0, GRID_W - NA_COLS)
    rel_col = cols[None, :] - win_start[:, None]
    col_mask = (rel_col >= 0) & (rel_col < NA_COLS)
    dc_idx = np.clip(cols[None, :] - cols[:, None] + NA_COLS - 1, 0, 2 * NA_COLS - 2)
    neg = jnp.finfo(F32).min
    bias_c = jnp.where(col_mask[None, None], rel_bias.astype(F32)[:, :, dc_idx], neg)
    slabs = jnp.stack([bias_c[:, min_start + s:min_start + s + kr] for s in range(n_slabs)], axis=1)
    slabs = jnp.transpose(slabs, (0, 1, 3, 2, 4))
    h = rel_bias.shape[0]
    return slabs.reshape(h, n_slabs, GRID_W, kr * GRID_W)


def _na_kernel(q_ref, k_ref, v_ref, kc_ref, vc_ref, bias_ref, o_ref, s_ctx_all, p_ctx_all, o_loc_all,
               *, heads_per_step, **geometry):
    for hh in range(heads_per_step):
        cols = slice(hh * NA_HEAD_DIM, (hh + 1) * NA_HEAD_DIM)
        _na_one_head(q_ref.at[0, :, cols], k_ref.at[0, :, cols], v_ref.at[0, :, cols], kc_ref.at[0, :, cols],
                     vc_ref.at[0, :, cols], bias_ref.at[hh], o_ref.at[0, :, cols],
                     s_ctx_all, p_ctx_all, o_loc_all, **geometry)


def _na_one_head(q_ref, k_ref, v_ref, kc_ref, vc_ref, bias_ref, o_ref, s_ctx_all, p_ctx_all, o_loc_all,
                 *, rows, rows_per_step, kr, min_start, scale):
    s_ctx_all[...] = lax.dot_general(q_ref[...], kc_ref[...], NT_DIMS, preferred_element_type=F32) * scale

    def scores(r):
        rs = jnp.clip(r - kr // 2, 0, rows - kr)
        slab = rs - r + (NA_ROWS_MAX - 1 - min_start)
        q0 = pl.multiple_of(r * GRID_W, GRID_W)
        k0 = pl.multiple_of(rs * GRID_W, GRID_W)
        qr = q_ref[pl.ds(q0, GRID_W), :]
        kw = k_ref[pl.ds(k0, kr * GRID_W), :]
        s_loc = lax.dot_general(qr, kw, NT_DIMS, preferred_element_type=F32) * scale + bias_ref[slab]
        return q0, k0, s_loc

    def softmax(q0, s_loc):
        s_ctx = s_ctx_all[pl.ds(q0, GRID_W), :]
        m = jnp.maximum(jnp.max(s_loc, axis=-1, keepdims=True), jnp.max(s_ctx, axis=-1, keepdims=True))
        p_loc = jnp.exp(s_loc - m)
        p_ctx = jnp.exp(s_ctx - m)
        inv = 1.0 / (jnp.sum(p_loc, axis=-1, keepdims=True) + jnp.sum(p_ctx, axis=-1, keepdims=True))
        return (p_loc * inv).astype(BF16), (p_ctx * inv).astype(BF16)

    def row_group(g, carry):
        sc = [scores(g * rows_per_step + u) for u in range(rows_per_step)]
        pr = [softmax(q0, s_loc) for q0, _, s_loc in sc]
        for (q0, k0, _), (p_loc, p_ctx) in zip(sc, pr):
            p_ctx_all[pl.ds(q0, GRID_W), :] = p_ctx
            o_loc_all[pl.ds(q0, GRID_W), :] = _dot(p_loc, v_ref[pl.ds(k0, kr * GRID_W), :])
        return carry

    lax.fori_loop(0, rows // rows_per_step, row_group, 0)
    o_ref[...] = (o_loc_all[...] + _dot(p_ctx_all[...], vc_ref[...])).astype(o_ref.dtype)


def _neighbourhood_attention(z3, zc3, bias_slabs, off_q, na_width):
    b, l, _ = z3.shape
    ctx_len = zc3.shape[1]
    dh = NA_HEAD_DIM
    heads = na_width // dh
    rows = l // GRID_W
    kr, min_start, n_slabs = _na_geometry(rows)
    hp = _tile(heads, NA_HEADS_PER_STEP)
    w = hp * dh
    qb, kb, vb = off_q // w, (off_q + na_width) // w, (off_q + 2 * na_width) // w
    kern = functools.partial(_na_kernel, heads_per_step=hp, rows=rows,
                             rows_per_step=_tile(rows, NA_ROWS_PER_STEP), kr=kr,
                             min_start=min_start, scale=dh ** -0.5)
    return pl.pallas_call(
        kern,
        grid=(heads // hp, b),
        in_specs=[pl.BlockSpec((1, l, w), lambda h, bi: (bi, 0, qb + h)),
                  pl.BlockSpec((1, l, w), lambda h, bi: (bi, 0, kb + h)),
                  pl.BlockSpec((1, l, w), lambda h, bi: (bi, 0, vb + h)),
                  pl.BlockSpec((1, ctx_len, w), lambda h, bi: (bi, 0, h)),
                  pl.BlockSpec((1, ctx_len, w), lambda h, bi: (bi, 0, heads // hp + h)),
                  pl.BlockSpec((hp, n_slabs, GRID_W, kr * GRID_W), lambda h, bi: (h, 0, 0, 0))],
        out_specs=pl.BlockSpec((1, l, w), lambda h, bi: (bi, 0, h)),
        out_shape=jax.ShapeDtypeStruct((b, l, na_width), BF16),
        scratch_shapes=[pltpu.VMEM((l, ctx_len), F32), pltpu.VMEM((l, ctx_len), BF16),
                        pltpu.VMEM((l, dh), F32)],
        compiler_params=_params(2),
    )(z3, z3, z3, zc3, zc3, bias_slabs)


def _merge_kernel(yf_ref, on_ref, wf_ref, wn_ref, gf_ref, gn_ref, o_ref, wf_s, wn_s):
    @pl.when(pl.program_id(1) == 0)
    def _():
        wf_s[...] = wf_ref[...].astype(BF16)
        wn_s[...] = wn_ref[...].astype(BF16)

    y_f = _dot(yf_ref[...], wf_s[...])
    y_n = _dot(on_ref[...], wn_s[...])
    o_ref[...] = (gf_ref[...].astype(F32) * y_f + gn_ref[...].astype(F32) * y_n).astype(o_ref.dtype)


def _merge(y_four, o_na, w_f, w_n, z, off_gate):
    m, kf = y_four.shape
    kn = o_na.shape[1]
    d = w_f.shape[1]
    tm = _tile(m, 1024)
    tn = _tile(np.gcd(d, off_gate), 512)
    gf0, gn0 = off_gate // tn, (off_gate + d) // tn
    return pl.pallas_call(
        _merge_kernel,
        grid=(d // tn, m // tm),
        in_specs=[pl.BlockSpec((tm, kf), lambda j, i: (i, 0)),
                  pl.BlockSpec((tm, kn), lambda j, i: (i, 0)),
                  pl.BlockSpec((kf, tn), lambda j, i: (0, j)),
                  pl.BlockSpec((kn, tn), lambda j, i: (0, j)),
                  pl.BlockSpec((tm, tn), lambda j, i: (i, gf0 + j)),
                  pl.BlockSpec((tm, tn), lambda j, i: (i, gn0 + j))],
        out_specs=pl.BlockSpec((tm, tn), lambda j, i: (i, j)),
        out_shape=jax.ShapeDtypeStruct((m, d), BF16),
        scratch_shapes=[pltpu.VMEM((kf, tn), BF16), pltpu.VMEM((kn, tn), BF16)],
        compiler_params=_params(2),
    )(y_four, o_na, w_f, w_n, z, z)


def _out_proj_kernel(a_ref, w_ref, x_ref, gt_ref, o_ref, w_s):
    @pl.when(pl.program_id(1) == 0)
    def _():
        w_s[...] = w_ref[...].astype(BF16)

    o_ref[...] = x_ref[...] + gt_ref[0] * _dot(a_ref[...], w_s[...])


def _out_proj_residual(a, w, x2d, mod, k_gate, seq):
    m, k = a.shape
    d = w.shape[1]
    tm = _tile(seq, 1024)
    tn = _tile(d, 512)
    gate0 = k_gate * (d // tn)
    return pl.pallas_call(
        _out_proj_kernel,
        grid=(d // tn, m // tm),
        in_specs=[pl.BlockSpec((tm, k), lambda j, i: (i, 0)),
                  pl.BlockSpec((k, tn), lambda j, i: (0, j)),
                  pl.BlockSpec((tm, tn), lambda j, i: (i, j)),
                  pl.BlockSpec((1, 1, tn), lambda j, i: (i * tm // seq, 0, gate0 + j))],
        out_specs=pl.BlockSpec((tm, tn), lambda j, i: (i, j)),
        out_shape=jax.ShapeDtypeStruct((m, d), F32),
        scratch_shapes=[pltpu.VMEM((k, tn), BF16)],
        compiler_params=_params(2),
    )(a, w, x2d, mod)


def _router_kernel(x_ref, g_ref, sc_ref, sh_ref, wr_ref, aff_ref, xp_ref):
    xn = _rms_modulate(x_ref[0], g_ref[...], sc_ref[0], sh_ref[0])
    half = xn.shape[1] // 2
    xp_ref[0] = _pack_bf16_pair(xn[:, :half], xn[:, half:])
    x_hi = xn.astype(BF16)
    x_lo = (xn - x_hi.astype(F32)).astype(BF16)
    w = wr_ref[...]
    w_hi = w.astype(BF16)
    w_lo = (w - w_hi.astype(F32)).astype(BF16)
    logits = (lax.dot_general(w_hi, x_hi, NT_DIMS, preferred_element_type=F32)
              + lax.dot_general(w_lo, x_hi, NT_DIMS, preferred_element_type=F32)
              + lax.dot_general(w_hi, x_lo, NT_DIMS, preferred_element_type=F32))
    ex = jnp.exp(logits - jnp.max(logits, axis=0, keepdims=True))
    aff_ref[0] = ex / jnp.sum(ex, axis=0, keepdims=True)


def _router(x, gain, mod, k_shift, k_scale, w_router_t):
    b, l, d = x.shape
    e = w_router_t.shape[0]
    tm = _tile(l, 256)
    return pl.pallas_call(
        _router_kernel,
        grid=(b, l // tm),
        in_specs=[pl.BlockSpec((1, tm, d), lambda bi, i: (bi, i, 0)),
                  pl.BlockSpec((1, d), lambda bi, i: (0, 0)),
                  pl.BlockSpec((1, 1, d), lambda bi, i: (bi, 0, k_scale)),
                  pl.BlockSpec((1, 1, d), lambda bi, i: (bi, 0, k_shift)),
                  pl.BlockSpec((e, d), lambda bi, i: (0, 0))],
        out_specs=[pl.BlockSpec((1, e, tm), lambda bi, i: (bi, 0, i)),
                   pl.BlockSpec((1, tm, d // 2), lambda bi, i: (bi, i, 0))],
        out_shape=[jax.ShapeDtypeStruct((b, e, l), F32), jax.ShapeDtypeStruct((b, l, d // 2), jnp.uint32)],
        compiler_params=_params(2),
    )(x, gain.reshape(1, d), mod, mod, w_router_t)


def _select_kernel(a_ref, tri_ref, same_ref, lower_ref, idx_ref, dst_ref, gate_ref, cum_ref, pos_ref, rank_ref,
                   *, cap):
    a = a_ref[...]
    n_rows, l = a.shape
    capf = float(cap)

    def count(mask):
        return jnp.sum(jnp.where(mask, 1.0, 0.0), axis=-1, keepdims=True)

    def halve(_, lo_hi):
        lo, hi = lo_hi
        mid = 0.5 * lo + 0.5 * hi
        ok = count(a_ref[...] >= mid) >= capf
        return jnp.where(ok, mid, lo), jnp.where(ok, hi, mid)

    _, hi = lax.fori_loop(0, BISECT_STEPS, halve,
                          (jnp.zeros((n_rows, 1), F32), jnp.full((n_rows, 1), 2.0, F32)))
    tau = jnp.max(jnp.where(a < hi, a, -1.0), axis=-1, keepdims=True)
    above = a > tau
    tied = a == tau
    need = capf - count(above)
    tri = tri_ref[...]
    tied_before = _dot(jnp.where(tied, 1.0, 0.0).astype(BF16), tri)
    chosen = above | (tied & (tied_before < need))
    chosen_b = jnp.where(chosen, 1.0, 0.0).astype(BF16)
    pos = _dot(chosen_b, tri)
    pos_ref[...] = jnp.where(chosen, pos, -1.0)
    per_token = _dot(same_ref[...], chosen_b)
    before_token = _dot(per_token.astype(BF16), tri)
    cum_ref[...] = before_token
    rank_ref[...] = before_token + _dot(lower_ref[...], chosen_b)

    tok = lax.broadcasted_iota(jnp.int32, (cap, l), 1).astype(F32)
    slot = lax.broadcasted_iota(jnp.int32, (cap, l), 0).astype(F32)
    lane = lax.broadcasted_iota(jnp.int32, (cap, n_rows), 1)

    def compact(r, accs):
        hit = pos_ref[pl.ds(r, 1), :] == slot
        picked = [jnp.sum(jnp.where(hit, v, 0.0), axis=-1, keepdims=True)
                  for v in (tok, rank_ref[pl.ds(r, 1), :], a_ref[pl.ds(r, 1), :])]
        return tuple(jnp.where(lane == r, p, acc) for p, acc in zip(picked, accs))

    zeros = jnp.zeros((cap, n_rows), F32)
    ids, ranks, gates = lax.fori_loop(0, n_rows, compact, (zeros, zeros, zeros))
    idx_ref[...] = ids.astype(jnp.int32)
    dst_ref[...] = ranks.astype(jnp.int32)
    gate_ref[...] = gates


def _select(aff_rows, cap, n_experts):
    n_rows, l = aff_rows.shape
    tri = jnp.asarray(np.triu(np.ones((l, l), np.float32), k=1), dtype=BF16)
    group = np.arange(n_rows) // n_experts
    same = group[:, None] == group[None, :]
    lower = same & (np.arange(n_rows)[None, :] < np.arange(n_rows)[:, None])
    full = lambda shape: pl.BlockSpec(shape, lambda i: (0, 0))
    slot_major = jax.ShapeDtypeStruct((cap, n_rows), jnp.int32)
    return pl.pallas_call(
        functools.partial(_select_kernel, cap=cap),
        grid=(1,),
        in_specs=[full((n_rows, l)), full((l, l)), full((n_rows, n_rows)), full((n_rows, n_rows))],
        out_specs=[full((cap, n_rows)), full((cap, n_rows)), full((cap, n_rows)), full((n_rows, l))],
        out_shape=[slot_major, slot_major, jax.ShapeDtypeStruct((cap, n_rows), F32),
                   jax.ShapeDtypeStruct((n_rows, l), F32)],
        scratch_shapes=[pltpu.VMEM((n_rows, l), F32), pltpu.VMEM((n_rows, l), F32)],
        compiler_params=_params(1),
    )(aff_rows, tri, jnp.asarray(same, dtype=BF16), jnp.asarray(lower, dtype=BF16))


def _gather_rows_kernel(rows_ref, src_hbm, o_ref, sem):
    n_rows = o_ref.shape[0]
    base = pl.program_id(0) * n_rows

    def start(j, carry):
        pltpu.make_async_copy(src_hbm.at[pl.ds(rows_ref[base + j], 1)], o_ref.at[pl.ds(j, 1)], sem).start()
        return carry

    lax.fori_loop(0, n_rows, start, 0, unroll=8)
    pltpu.make_async_copy(src_hbm.at[pl.ds(0, n_rows)], o_ref, sem).wait()


def _gather_rows(flat_rows, src):
    n = flat_rows.shape[0]
    w = src.shape[1]
    r = _tile(n, 1024)
    grid_spec = pltpu.PrefetchScalarGridSpec(
        num_scalar_prefetch=1,
        grid=(n // r,),
        in_specs=[pl.BlockSpec(memory_space=pl.ANY)],
        out_specs=pl.BlockSpec((r, w), lambda i, rows: (i, 0)),
        scratch_shapes=[pltpu.SemaphoreType.DMA],
    )
    return pl.pallas_call(
        _gather_rows_kernel,
        grid_spec=grid_spec,
        out_shape=jax.ShapeDtypeStruct((n, w), src.dtype),
        compiler_params=_params(1),
    )(flat_rows, src)


def _ffn_up_kernel(xp_ref, w1_ref, w3_ref, o_ref, x_s):
    half = xp_ref.shape[1]

    @pl.when(pl.program_id(2) == 0)
    def _():
        lo, hi = _unpack_bf16_pair(xp_ref[...])
        x_s[:, :half] = lo
        x_s[:, half:] = hi

    xin = x_s[...]
    a = _dot(xin, w1_ref[0].astype(BF16))
    o_ref[...] = (a * jax.nn.sigmoid(a) * _dot(xin, w3_ref[0].astype(BF16))).astype(o_ref.dtype)


def _ffn_up(xin_packed, w1, w3):
    n, half = xin_packed.shape
    e, d, ff = w1.shape
    rows_e = n // e
    tm = _tile(rows_e, 1024)
    tn = _tile(ff, 256)
    per_e = rows_e // tm
    return pl.pallas_call(
        _ffn_up_kernel,
        grid=(e, per_e, ff // tn),
        in_specs=[pl.BlockSpec((tm, half), lambda ei, i, j: (ei * per_e + i, 0)),
                  pl.BlockSpec((1, d, tn), lambda ei, i, j: (ei, 0, j)),
                  pl.BlockSpec((1, d, tn), lambda ei, i, j: (ei, 0, j))],
        out_specs=pl.BlockSpec((tm, tn), lambda ei, i, j: (ei * per_e + i, j)),
        out_shape=jax.ShapeDtypeStruct((n, ff), BF16),
        scratch_shapes=[pltpu.VMEM((tm, d), BF16)],
        compiler_params=_params(3),
    )(xin_packed, w1, w3)


def _bf16_bits(x):
    return lax.bitcast_convert_type(x.astype(BF16).astype(F32), jnp.uint32)


def _pack_bf16_pair(lo, hi):
    return (_bf16_bits(lo) >> 16) | _bf16_bits(hi)


def _unpack_bf16_pair(words):
    lo = lax.bitcast_convert_type(words << 16, F32).astype(BF16)
    hi = lax.bitcast_convert_type(words & jnp.uint32(0xFFFF0000), F32).astype(BF16)
    return lo, hi


def _ffn_down_kernel(h_ref, wa_ref, wb_ref, g_ref, o_ref):
    h = h_ref[...]
    gate = g_ref[...]
    o_ref[...] = _pack_bf16_pair(_dot(h, wa_ref[0].astype(BF16)) * gate,
                                 _dot(h, wb_ref[0].astype(BF16)) * gate)


def _ffn_down(hid, w2, gate_col):
    n, ff = hid.shape
    e, _, d = w2.shape
    rows_e = n // e
    half = d // 2
    tw = _tile(half, 256)
    nb = half // tw
    return pl.pallas_call(
        _ffn_down_kernel,
        grid=(e, nb),
        in_specs=[pl.BlockSpec((rows_e, ff), lambda ei, j: (ei, 0)),
                  pl.BlockSpec((1, ff, tw), lambda ei, j: (ei, 0, j)),
                  pl.BlockSpec((1, ff, tw), lambda ei, j: (ei, 0, nb + j)),
                  pl.BlockSpec((rows_e, 1), lambda ei, j: (ei, 0))],
        out_specs=pl.BlockSpec((rows_e, tw), lambda ei, j: (ei, j)),
        out_shape=jax.ShapeDtypeStruct((n, half), jnp.uint32),
        compiler_params=_params(2),
    )(hid, w2, w2, gate_col)


def _row_copy(src, dst_hbm, sem, j, row):
    return pltpu.make_async_copy(src.at[pl.ds(j, 1)], dst_hbm.at[pl.ds(row, 1)], sem)


def _permute_kernel(dest_ref, src_ref, dst_hbm, sem):
    n_rows = src_ref.shape[0]
    base = pl.program_id(0) * n_rows

    def start(j, carry):
        _row_copy(src_ref, dst_hbm, sem, j, dest_ref[base + j]).start()
        return carry

    lax.fori_loop(0, n_rows, start, 0, unroll=8)
    pltpu.make_async_copy(src_ref, dst_hbm.at[pl.ds(0, n_rows)], sem).wait()


def _permute_rows(rows, dest):
    n, w = rows.shape
    r = _tile(n, 1024)
    grid_spec = pltpu.PrefetchScalarGridSpec(
        num_scalar_prefetch=1,
        grid=(n // r,),
        in_specs=[pl.BlockSpec((r, w), lambda i, dest_ref: (i, 0))],
        out_specs=pl.BlockSpec(memory_space=pl.ANY),
        scratch_shapes=[pltpu.SemaphoreType.DMA],
    )
    return pl.pallas_call(
        _permute_kernel,
        grid_spec=grid_spec,
        out_shape=jax.ShapeDtypeStruct((n, w), rows.dtype),
        compiler_params=_params(1),
    )(dest, rows)


def _combine_kernel(start_ref, nchunks_ref, first_ref, rows_hbm, lo_ref, hi_ref, x_ref, gt_ref, fg_ref,
                    o_ref, buf, sem, acc_lo, acc_hi, *, n_total):
    step = pl.program_id(0) * pl.num_programs(1) + pl.program_id(1)
    n_steps = pl.num_programs(0) * pl.num_programs(1)
    chunk = buf.shape[1]
    tm = acc_lo.shape[0]
    n_chunks = nchunks_ref[step]
    first = first_ref[step]

    def row0(s, c):
        return pl.multiple_of(jnp.minimum(start_ref[s] + c * chunk, n_total - chunk), SUBLANES)

    def fetch(s, c, slot):
        return pltpu.make_async_copy(rows_hbm.at[pl.ds(row0(s, c), chunk)], buf.at[slot], sem.at[slot])

    @pl.when(step == 0)
    def _():
        fetch(0, 0, 0).start()

    acc_lo[...] = jnp.zeros_like(acc_lo)
    acc_hi[...] = jnp.zeros_like(acc_hi)
    row_lo = lo_ref[0]
    row_hi = hi_ref[0]

    def consume(c, carry):
        slot = (first + c) % 2
        ends_step = c + 1 == n_chunks

        @pl.when(jnp.logical_not(ends_step))
        def _():
            fetch(step, c + 1, 1 - slot).start()

        @pl.when(jnp.logical_and(ends_step, step + 1 < n_steps))
        def _():
            fetch(step + 1, 0, 1 - slot).start()

        fetch(step, c, slot).wait()
        lo, hi = _unpack_bf16_pair(buf[slot])
        row = (row0(step, c) + lax.broadcasted_iota(jnp.int32, (tm, chunk), 1)).astype(F32)
        first_row = jnp.maximum(row_lo, (start_ref[step] + c * chunk).astype(F32))
        onehot = jnp.where(jnp.logical_and(row >= first_row, row < row_hi), 1.0, 0.0).astype(BF16)
        acc_lo[...] += _dot(onehot, lo)
        acc_hi[...] += _dot(onehot, hi)
        return carry

    lax.fori_loop(0, n_chunks, consume, 0)
    y = jnp.concatenate([acc_lo[...], acc_hi[...]], axis=-1)
    x2 = x_ref[0] + gt_ref[0] * y
    ms = jnp.mean(x2 * x2, axis=-1, keepdims=True)
    o_ref[0] = x2 * lax.rsqrt(ms + RMS_EPS) * fg_ref[...]


def _combine(sorted_rows, row_lo, row_hi, x, mod, k_gate, final_g):
    b, l, d = x.shape
    n_total, half = sorted_rows.shape
    tm = _tile(l, 256)
    nt = l // tm
    chunk = COMBINE_CHUNK
    assert n_total >= chunk and n_total % SUBLANES == 0
    tile_lo = row_lo[:, ::tm].astype(jnp.int32).reshape(-1)
    tile_hi = row_hi[:, tm - 1::tm].astype(jnp.int32).reshape(-1)
    start = tile_lo // SUBLANES * SUBLANES
    n_chunks = jnp.maximum(1, (tile_hi - start + chunk - 1) // chunk)
    first = jnp.cumsum(n_chunks) - n_chunks
    col = lambda a: a.reshape(b, l, 1)
    grid_spec = pltpu.PrefetchScalarGridSpec(
        num_scalar_prefetch=3,
        grid=(b, nt),
        in_specs=[pl.BlockSpec(memory_space=pl.ANY),
                  pl.BlockSpec((1, tm, 1), lambda bi, i, *_: (bi, i, 0)),
                  pl.BlockSpec((1, tm, 1), lambda bi, i, *_: (bi, i, 0)),
                  pl.BlockSpec((1, tm, d), lambda bi, i, *_: (bi, i, 0)),
                  pl.BlockSpec((1, 1, d), lambda bi, i, *_: (bi, 0, k_gate)),
                  pl.BlockSpec((1, d), lambda bi, i, *_: (0, 0))],
        out_specs=pl.BlockSpec((1, tm, d), lambda bi, i, *_: (bi, i, 0)),
        scratch_shapes=[pltpu.VMEM((2, chunk, half), jnp.uint32), pltpu.SemaphoreType.DMA((2,)),
                        pltpu.VMEM((tm, half), F32), pltpu.VMEM((tm, half), F32)],
    )
    return pl.pallas_call(
        functools.partial(_combine_kernel, n_total=n_total),
        grid_spec=grid_spec,
        out_shape=jax.ShapeDtypeStruct((b, l, d), F32),
        compiler_params=_params(2),
    )(start, n_chunks.astype(jnp.int32), first.astype(jnp.int32), sorted_rows, col(row_lo), col(row_hi),
      x, mod, final_g.reshape(1, d))


def kernel(x, c, ctx, c_ctx, w_mod, b_mod, norm_mix_g, w_in, b_gate, w_fourier, na_rel_bias,
           w_na_out, w_out, norm_ffn_g, w_router, w1, w3, w2, final_norm_g):
    assert w_mod.shape[0] == 1, "single-layer stack only"
    b, l, d = x.shape
    ctx_len = ctx.shape[1]
    f_width = w_fourier.shape[1]
    na_width = w_na_out.shape[1]
    n_experts = w_router.shape[-1]
    in_width = w_in.shape[-1]
    off_q = f_width
    off_k = off_q + na_width
    off_g = off_k + 2 * na_width
    cap = EC_FACTOR * l // n_experts
    SH_M, SC_M, GT_M, SH_F, SC_F, GT_F = range(N_MOD)

    pad = (-(b + 1)) % 16
    cond = jnp.concatenate([c, c_ctx[None], jnp.zeros((pad, d), F32)], axis=0)
    mod_all = _ada_params(cond, w_mod[0], b_mod[0])
    mod = mod_all[:b].reshape(b, 1, N_MOD * d)
    mod_ctx = mod_all[b:b + 1].reshape(1, 1, N_MOD * d)

    w_in_bf = w_in[0].astype(BF16)
    xn = _norm_mod(x, norm_mix_g[0], mod, SH_M, SC_M)
    cn = _norm_mod(ctx, norm_mix_g[0], mod_ctx, SH_M, SC_M)
    b_full = jnp.concatenate([jnp.zeros((off_g,), F32), b_gate[0]]).reshape(1, in_width)
    z = _proj_in(xn.reshape(b * l, d), w_in_bf, b_full, off_g)
    zc = _matmul_cols(cn.reshape(b * ctx_len, d), w_in_bf, off_k, 2 * na_width)
    z3 = z.reshape(b, l, in_width)

    ab = _fourier_chan(z3, f_width)
    y_four = _fourier_pos(ab.reshape(b, 2 * l, f_width))
    bias_slabs = _na_bias_slabs(na_rel_bias[0], l // GRID_W)
    o_na = _neighbourhood_attention(z3, zc.reshape(b, ctx_len, 2 * na_width), bias_slabs, off_q, na_width)
    mixed = _merge(y_four.reshape(b * l, f_width), o_na.reshape(b * l, na_width),
                   w_fourier[0], w_na_out[0], z, off_g)
    x1 = _out_proj_residual(mixed, w_out[0], x.reshape(b * l, d), mod, GT_M, l)

    x1_3 = x1.reshape(b, l, d)
    aff_t, xn_packed = _router(x1_3, norm_ffn_g[0], mod, SH_F, SC_F, w_router[0].T)
    idx_t, dst_t, gate_t, cum = _select(aff_t.reshape(b * n_experts, l), cap, n_experts)

    def expert_major(slot_major, batch_stride):
        v = slot_major.T.reshape(b, n_experts, cap)
        offs = (jnp.arange(b, dtype=jnp.int32) * batch_stride)[None, :, None]
        return (jnp.transpose(v, (1, 0, 2)) + offs.astype(v.dtype)).reshape(-1)

    per_batch = n_experts * cap
    xin = _gather_rows(expert_major(idx_t, l), xn_packed.reshape(b * l, d // 2))
    hid = _ffn_up(xin, w1[0], w3[0])
    out = _ffn_down(hid, w2[0], expert_major(gate_t, 0).reshape(-1, 1))
    sorted_rows = _permute_rows(out, expert_major(dst_t, per_batch))
    before = cum.reshape(b, n_experts, l)[:, 0, :] + (jnp.arange(b, dtype=F32) * per_batch)[:, None]
    after = jnp.concatenate([before[:, 1:], before[:, :1] + per_batch], axis=1)
    return _combine(sorted_rows, before, after, x1_3, mod, GT_F, final_norm_g)
```

```python
import functools

import numpy as np
import jax
import jax.numpy as jnp
from jax import lax
from jax.experimental import pallas as pl
from jax.experimental.pallas import tpu as pltpu

GRID_W = 64
F_GROUPS = 4
NA_HEAD_DIM = 128
NA_ROWS_MAX = 8
NA_COLS = 16
EC_FACTOR = 2
N_MOD = 6
RMS_EPS = 1e-6

SUBLANES = 8
COMBINE_CHUNK = 256
VMEM_LIMIT_BYTES = 56 * 1024 * 1024
BISECT_STEPS = 160
NA_ROWS_PER_STEP = 32
NA_HEADS_PER_STEP = 4
DMA_ISSUE_UNROLL = 32

F32 = jnp.float32
BF16 = jnp.bfloat16
NT_DIMS = (((1,), (1,)), ((), ()))


def _params(n_axes):
    return pltpu.CompilerParams(
        dimension_semantics=("arbitrary",) * n_axes, vmem_limit_bytes=VMEM_LIMIT_BYTES)


def _tile(dim, pref):
    t = min(dim, pref)
    while dim % t:
        t -= 1
    return t


def _dot(a, b):
    return jnp.dot(a, b, preferred_element_type=F32)


def _rms_modulate(x, gain, scale, shift):
    ms = jnp.mean(x * x, axis=-1, keepdims=True)
    y = x * lax.rsqrt(ms + RMS_EPS) * gain
    return y * (1.0 + scale) + shift


def _ada_kernel(c_ref, w_ref, b_ref, o_ref):
    cond = c_ref[...]
    act = cond * jax.nn.sigmoid(cond)
    o_ref[...] = _dot(act.astype(BF16), w_ref[...].astype(BF16)) + b_ref[...]


def _ada_params(cond, w_mod, b_mod):
    m, d = cond.shape
    n = w_mod.shape[1]
    tn = _tile(n, 512)
    return pl.pallas_call(
        _ada_kernel,
        grid=(n // tn,),
        in_specs=[pl.BlockSpec((m, d), lambda j: (0, 0)),
                  pl.BlockSpec((d, tn), lambda j: (0, j)),
                  pl.BlockSpec((1, tn), lambda j: (0, j))],
        out_specs=pl.BlockSpec((m, tn), lambda j: (0, j)),
        out_shape=jax.ShapeDtypeStruct((m, n), F32),
        compiler_params=_params(1),
    )(cond, w_mod, b_mod.reshape(1, n))


def _norm_mod_kernel(x_ref, g_ref, sc_ref, sh_ref, o_ref):
    o_ref[0] = _rms_modulate(x_ref[0], g_ref[...], sc_ref[0], sh_ref[0]).astype(o_ref.dtype)


def _norm_mod(x, gain, mod, k_shift, k_scale):
    b, l, d = x.shape
    per_batch = mod.shape[0] == b
    tm = _tile(l, 512)
    bsel = (lambda i: i) if per_batch else (lambda i: 0)
    return pl.pallas_call(
        _norm_mod_kernel,
        grid=(b, l // tm),
        in_specs=[pl.BlockSpec((1, tm, d), lambda bi, i: (bi, i, 0)),
                  pl.BlockSpec((1, d), lambda bi, i: (0, 0)),
                  pl.BlockSpec((1, 1, d), lambda bi, i: (bsel(bi), 0, k_scale)),
                  pl.BlockSpec((1, 1, d), lambda bi, i: (bsel(bi), 0, k_shift))],
        out_specs=pl.BlockSpec((1, tm, d), lambda bi, i: (bi, i, 0)),
        out_shape=jax.ShapeDtypeStruct((b, l, d), BF16),
        compiler_params=_params(2),
    )(x, gain.reshape(1, d), mod, mod)


def _proj_in_kernel(a_ref, w_ref, b_ref, o_ref, *, gate_block0):
    acc = _dot(a_ref[...], w_ref[...])
    j = pl.program_id(1)

    @pl.when(j < gate_block0)
    def _():
        o_ref[...] = acc.astype(o_ref.dtype)

    @pl.when(j >= gate_block0)
    def _():
        o_ref[...] = jax.nn.sigmoid(acc + b_ref[...]).astype(o_ref.dtype)


def _proj_in(a, w, b_full, off_gate):
    m, k = a.shape
    n = w.shape[1]
    tm = _tile(m, 1024)
    tn = _tile(np.gcd(n, off_gate), 1024)
    return pl.pallas_call(
        functools.partial(_proj_in_kernel, gate_block0=off_gate // tn),
        grid=(m // tm, n // tn),
        in_specs=[pl.BlockSpec((tm, k), lambda i, j: (i, 0)),
                  pl.BlockSpec((k, tn), lambda i, j: (0, j)),
                  pl.BlockSpec((1, tn), lambda i, j: (0, j))],
        out_specs=pl.BlockSpec((tm, tn), lambda i, j: (i, j)),
        out_shape=jax.ShapeDtypeStruct((m, n), BF16),
        compiler_params=_params(2),
    )(a, w, b_full)


def _mm_kernel(a_ref, w_ref, o_ref):
    o_ref[...] = _dot(a_ref[...], w_ref[...]).astype(o_ref.dtype)


def _matmul_cols(a, w, col0, ncols):
    m, k = a.shape
    tm = _tile(m, 1024)
    tn = _tile(np.gcd(ncols, col0) if col0 else ncols, 512)
    j0 = col0 // tn
    return pl.pallas_call(
        _mm_kernel,
        grid=(m // tm, ncols // tn),
        in_specs=[pl.BlockSpec((tm, k), lambda i, j: (i, 0)),
                  pl.BlockSpec((k, tn), lambda i, j: (0, j0 + j))],
        out_specs=pl.BlockSpec((tm, tn), lambda i, j: (i, j)),
        out_shape=jax.ShapeDtypeStruct((m, ncols), BF16),
        compiler_params=_params(2),
    )(a, w)


def _dft_tables(n, scale):
    k = np.arange(n)
    ang = 2.0 * np.pi * ((k[:, None] * k[None, :]) % n) / n
    return np.cos(ang) * scale, np.sin(ang) * scale


def _fourier_chan_kernel(u_ref, t_ref, o_ref):
    dg = u_ref.shape[-1]
    res = _dot(u_ref[0], t_ref[...])
    o_ref[0, 0] = res[:, :dg].astype(o_ref.dtype)
    o_ref[0, 1] = res[:, dg:].astype(o_ref.dtype)


def _fourier_chan(z3, f_width):
    b, l, _ = z3.shape
    dg = f_width // F_GROUPS
    cd, sd = _dft_tables(dg, dg ** -0.5)
    table = jnp.asarray(np.concatenate([cd, sd], axis=1), dtype=BF16)
    tm = _tile(l, 1024)
    return pl.pallas_call(
        _fourier_chan_kernel,
        grid=(b, l // tm, F_GROUPS),
        in_specs=[pl.BlockSpec((1, tm, dg), lambda bi, i, g: (bi, i, g)),
                  pl.BlockSpec((dg, 2 * dg), lambda bi, i, g: (0, 0))],
        out_specs=pl.BlockSpec((1, 2, tm, dg), lambda bi, i, g: (bi, 0, i, g)),
        out_shape=jax.ShapeDtypeStruct((b, 2, l, f_width), BF16),
        compiler_params=_params(3),
    )(z3, table)


def _fourier_pos_kernel(t_ref, ab_ref, o_ref):
    o_ref[0] = _dot(t_ref[...], ab_ref[0]).astype(o_ref.dtype)


def _fourier_pos(ab):
    b, l2, fw = ab.shape
    l = l2 // 2
    cl, sl = _dft_tables(l, l ** -0.5)
    table = jnp.asarray(np.concatenate([cl, -sl], axis=1), dtype=BF16)
    tm = _tile(l, 1024)
    tn = _tile(fw, 512)
    return pl.pallas_call(
        _fourier_pos_kernel,
        grid=(b, fw // tn, l // tm),
        in_specs=[pl.BlockSpec((tm, l2), lambda bi, j, i: (i, 0)),
                  pl.BlockSpec((1, l2, tn), lambda bi, j, i: (bi, 0, j))],
        out_specs=pl.BlockSpec((1, tm, tn), lambda bi, j, i: (bi, i, j)),
        out_shape=jax.ShapeDtypeStruct((b, l, fw), BF16),
        compiler_params=_params(3),
    )(table, ab)


def _na_geometry(rows):
    kr = min(NA_ROWS_MAX, rows)
    starts = [int(np.clip(r - kr // 2, 0, rows - kr)) - r + NA_ROWS_MAX - 1 for r in range(rows)]
    return kr, min(starts), max(starts) - min(starts) + 1


def _na_bias_slabs(rel_bias, rows):
    kr, min_start, n_slabs = _na_geometry(rows)
    cols = np.arange(GRID_W)
    win_start = np.clip(cols - NA_COLS // 2, 0, GRID_W - NA_COLS)
    rel_col = cols[None, :] - win_start[:, None]
    col_mask = (rel_col >= 0) & (rel_col < NA_COLS)
    dc_idx = np.clip(cols[None, :] - cols[:, None] + NA_COLS - 1, 0, 2 * NA_COLS - 2)
    neg = jnp.finfo(F32).min
    row_idx = min_start + np.arange(n_slabs)[:, None] + np.arange(kr)[None, :]
    slabs = rel_bias.astype(F32)[:, row_idx[:, None, :, None], dc_idx[None, :, None, :]]
    slabs = jnp.where(col_mask[None, None, :, None, :], slabs, neg)
    h = rel_bias.shape[0]
    return slabs.reshape(h, n_slabs, GRID_W, kr * GRID_W)


def _na_kernel(q_ref, k_ref, v_ref, kc_ref, vc_ref, bias_ref, o_ref, s_ctx_all, p_ctx_all, o_loc_all,
               *, heads_per_step, **geometry):
    for hh in range(heads_per_step):
        cols = slice(hh * NA_HEAD_DIM, (hh + 1) * NA_HEAD_DIM)
        _na_one_head(q_ref.at[0, :, cols], k_ref.at[0, :, cols], v_ref.at[0, :, cols], kc_ref.at[0, :, cols],
                     vc_ref.at[0, :, cols], bias_ref.at[hh], o_ref.at[0, :, cols],
                     s_ctx_all, p_ctx_all, o_loc_all, **geometry)


def _na_one_head(q_ref, k_ref, v_ref, kc_ref, vc_ref, bias_ref, o_ref, s_ctx_all, p_ctx_all, o_loc_all,
                 *, rows, rows_per_step, kr, min_start, scale):
    s_ctx_all[...] = lax.dot_general(q_ref[...], kc_ref[...], NT_DIMS, preferred_element_type=F32) * scale

    def scores(r):
        rs = jnp.clip(r - kr // 2, 0, rows - kr)
        slab = rs - r + (NA_ROWS_MAX - 1 - min_start)
        q0 = pl.multiple_of(r * GRID_W, GRID_W)
        k0 = pl.multiple_of(rs * GRID_W, GRID_W)
        qr = q_ref[pl.ds(q0, GRID_W), :]
        kw = k_ref[pl.ds(k0, kr * GRID_W), :]
        s_loc = lax.dot_general(qr, kw, NT_DIMS, preferred_element_type=F32) * scale + bias_ref[slab]
        return q0, k0, s_loc

    def softmax(q0, s_loc):
        s_ctx = s_ctx_all[pl.ds(q0, GRID_W), :]
        m = jnp.maximum(jnp.max(s_loc, axis=-1, keepdims=True), jnp.max(s_ctx, axis=-1, keepdims=True))
        p_loc = jnp.exp(s_loc - m)
        p_ctx = jnp.exp(s_ctx - m)
        inv = 1.0 / (jnp.sum(p_loc, axis=-1, keepdims=True) + jnp.sum(p_ctx, axis=-1, keepdims=True))
        return (p_loc * inv).astype(BF16), (p_ctx * inv).astype(BF16)

    def row_group(g, carry):
        sc = [scores(g * rows_per_step + u) for u in range(rows_per_step)]
        pr = [softmax(q0, s_loc) for q0, _, s_loc in sc]
        for (q0, k0, _), (p_loc, p_ctx) in zip(sc, pr):
            p_ctx_all[pl.ds(q0, GRID_W), :] = p_ctx
            o_loc_all[pl.ds(q0, GRID_W), :] = _dot(p_loc, v_ref[pl.ds(k0, kr * GRID_W), :])
        return carry

    lax.fori_loop(0, rows // rows_per_step, row_group, 0)
    o_ref[...] = (o_loc_all[...] + _dot(p_ctx_all[...], vc_ref[...])).astype(o_ref.dtype)


def _neighbourhood_attention(z3, zc3, bias_slabs, off_q, na_width):
    b, l, _ = z3.shape
    ctx_len = zc3.shape[1]
    dh = NA_HEAD_DIM
    heads = na_width // dh
    rows = l // GRID_W
    kr, min_start, n_slabs = _na_geometry(rows)
    hp = _tile(heads, NA_HEADS_PER_STEP)
    w = hp * dh
    qb, kb, vb = off_q // w, (off_q + na_width) // w, (off_q + 2 * na_width) // w
    kern = functools.partial(_na_kernel, heads_per_step=hp, rows=rows,
                             rows_per_step=_tile(rows, NA_ROWS_PER_STEP), kr=kr,
                             min_start=min_start, scale=dh ** -0.5)
    return pl.pallas_call(
        kern,
        grid=(heads // hp, b),
        in_specs=[pl.BlockSpec((1, l, w), lambda h, bi: (bi, 0, qb + h)),
                  pl.BlockSpec((1, l, w), lambda h, bi: (bi, 0, kb + h)),
                  pl.BlockSpec((1, l, w), lambda h, bi: (bi, 0, vb + h)),
                  pl.BlockSpec((1, ctx_len, w), lambda h, bi: (bi, 0, h)),
                  pl.BlockSpec((1, ctx_len, w), lambda h, bi: (bi, 0, heads // hp + h)),
                  pl.BlockSpec((hp, n_slabs, GRID_W, kr * GRID_W), lambda h, bi: (h, 0, 0, 0))],
        out_specs=pl.BlockSpec((1, l, w), lambda h, bi: (bi, 0, h)),
        out_shape=jax.ShapeDtypeStruct((b, l, na_width), BF16),
        scratch_shapes=[pltpu.VMEM((l, ctx_len), F32), pltpu.VMEM((l, ctx_len), BF16),
                        pltpu.VMEM((l, dh), F32)],
        compiler_params=_params(2),
    )(z3, z3, z3, zc3, zc3, bias_slabs)


def _merge_kernel(yf_ref, on_ref, wf_ref, wn_ref, gf_ref, gn_ref, o_ref, wf_s, wn_s):
    @pl.when(pl.program_id(1) == 0)
    def _():
        wf_s[...] = wf_ref[...].astype(BF16)
        wn_s[...] = wn_ref[...].astype(BF16)

    y_f = _dot(yf_ref[...], wf_s[...])
    y_n = _dot(on_ref[...], wn_s[...])
    o_ref[...] = (gf_ref[...].astype(F32) * y_f + gn_ref[...].astype(F32) * y_n).astype(o_ref.dtype)


def _merge(y_four, o_na, w_f, w_n, z, off_gate):
    m, kf = y_four.shape
    kn = o_na.shape[1]
    d = w_f.shape[1]
    tm = _tile(m, 1024)
    tn = _tile(np.gcd(d, off_gate), 512)
    gf0, gn0 = off_gate // tn, (off_gate + d) // tn
    return pl.pallas_call(
        _merge_kernel,
        grid=(d // tn, m // tm),
        in_specs=[pl.BlockSpec((tm, kf), lambda j, i: (i, 0)),
                  pl.BlockSpec((tm, kn), lambda j, i: (i, 0)),
                  pl.BlockSpec((kf, tn), lambda j, i: (0, j)),
                  pl.BlockSpec((kn, tn), lambda j, i: (0, j)),
                  pl.BlockSpec((tm, tn), lambda j, i: (i, gf0 + j)),
                  pl.BlockSpec((tm, tn), lambda j, i: (i, gn0 + j))],
        out_specs=pl.BlockSpec((tm, tn), lambda j, i: (i, j)),
        out_shape=jax.ShapeDtypeStruct((m, d), BF16),
        scratch_shapes=[pltpu.VMEM((kf, tn), BF16), pltpu.VMEM((kn, tn), BF16)],
        compiler_params=_params(2),
    )(y_four, o_na, w_f, w_n, z, z)


def _out_proj_kernel(a_ref, w_ref, x_ref, gt_ref, o_ref, w_s):
    @pl.when(pl.program_id(1) == 0)
    def _():
        w_s[...] = w_ref[...].astype(BF16)

    o_ref[...] = x_ref[...] + gt_ref[0] * _dot(a_ref[...], w_s[...])


def _out_proj_residual(a, w, x2d, mod, k_gate, seq):
    m, k = a.shape
    d = w.shape[1]
    tm = _tile(seq, 1024)
    tn = _tile(d, 512)
    gate0 = k_gate * (d // tn)
    return pl.pallas_call(
        _out_proj_kernel,
        grid=(d // tn, m // tm),
        in_specs=[pl.BlockSpec((tm, k), lambda j, i: (i, 0)),
                  pl.BlockSpec((k, tn), lambda j, i: (0, j)),
                  pl.BlockSpec((tm, tn), lambda j, i: (i, j)),
                  pl.BlockSpec((1, 1, tn), lambda j, i: (i * tm // seq, 0, gate0 + j))],
        out_specs=pl.BlockSpec((tm, tn), lambda j, i: (i, j)),
        out_shape=jax.ShapeDtypeStruct((m, d), F32),
        scratch_shapes=[pltpu.VMEM((k, tn), BF16)],
        compiler_params=_params(2),
    )(a, w, x2d, mod)


def _router_kernel(x_ref, g_ref, sc_ref, sh_ref, wr_ref, aff_ref, xp_ref):
    xn = _rms_modulate(x_ref[0], g_ref[...], sc_ref[0], sh_ref[0])
    half = xn.shape[1] // 2
    xp_ref[0] = _pack_bf16_pair(xn[:, :half], xn[:, half:])
    x_hi = xn.astype(BF16)
    x_lo = (xn - x_hi.astype(F32)).astype(BF16)
    w = wr_ref[...]
    w_hi = w.astype(BF16)
    w_lo = (w - w_hi.astype(F32)).astype(BF16)
    logits = (lax.dot_general(w_hi, x_hi, NT_DIMS, preferred_element_type=F32)
              + lax.dot_general(w_lo, x_hi, NT_DIMS, preferred_element_type=F32)
              + lax.dot_general(w_hi, x_lo, NT_DIMS, preferred_element_type=F32))
    ex = jnp.exp(logits - jnp.max(logits, axis=0, keepdims=True))
    aff_ref[0] = ex / jnp.sum(ex, axis=0, keepdims=True)


def _router(x, gain, mod, k_shift, k_scale, w_router_t):
    b, l, d = x.shape
    e = w_router_t.shape[0]
    tm = _tile(l, 256)
    return pl.pallas_call(
        _router_kernel,
        grid=(b, l // tm),
        in_specs=[pl.BlockSpec((1, tm, d), lambda bi, i: (bi, i, 0)),
                  pl.BlockSpec((1, d), lambda bi, i: (0, 0)),
                  pl.BlockSpec((1, 1, d), lambda bi, i: (bi, 0, k_scale)),
                  pl.BlockSpec((1, 1, d), lambda bi, i: (bi, 0, k_shift)),
                  pl.BlockSpec((e, d), lambda bi, i: (0, 0))],
        out_specs=[pl.BlockSpec((1, e, tm), lambda bi, i: (bi, 0, i)),
                   pl.BlockSpec((1, tm, d // 2), lambda bi, i: (bi, i, 0))],
        out_shape=[jax.ShapeDtypeStruct((b, e, l), F32), jax.ShapeDtypeStruct((b, l, d // 2), jnp.uint32)],
        compiler_params=_params(2),
    )(x, gain.reshape(1, d), mod, mod, w_router_t)


def _select_kernel(a_ref, tri_ref, same_ref, lower_ref, idx_ref, dst_ref, gate_ref, cum_ref, pos_ref, rank_ref,
                   *, cap):
    a = a_ref[...]
    n_rows, l = a.shape
    capf = float(cap)

    def count(mask):
        return jnp.sum(jnp.where(mask, 1.0, 0.0), axis=-1, keepdims=True)

    def halve(_, lo_hi):
        lo, hi = lo_hi
        mid = 0.5 * lo + 0.5 * hi
        ok = count(a_ref[...] >= mid) >= capf
        return jnp.where(ok, mid, lo), jnp.where(ok, hi, mid)

    _, hi = lax.fori_loop(0, BISECT_STEPS, halve,
                          (jnp.zeros((n_rows, 1), F32), jnp.full((n_rows, 1), 2.0, F32)))
    tau = jnp.max(jnp.where(a < hi, a, -1.0), axis=-1, keepdims=True)
    above = a > tau
    tied = a == tau
    need = capf - count(above)
    tri = tri_ref[...]
    tied_before = _dot(jnp.where(tied, 1.0, 0.0).astype(BF16), tri)
    chosen = above | (tied & (tied_before < need))
    chosen_b = jnp.where(chosen, 1.0, 0.0).astype(BF16)
    pos = _dot(chosen_b, tri)
    pos_ref[...] = jnp.where(chosen, pos, -1.0)
    per_token = _dot(same_ref[...], chosen_b)
    before_token = _dot(per_token.astype(BF16), tri)
    cum_ref[...] = before_token
    rank_ref[...] = before_token + _dot(lower_ref[...], chosen_b)

    tok = lax.broadcasted_iota(jnp.int32, (cap, l), 1).astype(F32)
    slot = lax.broadcasted_iota(jnp.int32, (cap, l), 0).astype(F32)
    lane = lax.broadcasted_iota(jnp.int32, (cap, n_rows), 1)

    def compact(r, accs):
        hit = pos_ref[pl.ds(r, 1), :] == slot
        picked = [jnp.sum(jnp.where(hit, v, 0.0), axis=-1, keepdims=True)
                  for v in (tok, rank_ref[pl.ds(r, 1), :], a_ref[pl.ds(r, 1), :])]
        return tuple(jnp.where(lane == r, p, acc) for p, acc in zip(picked, accs))

    zeros = jnp.zeros((cap, n_rows), F32)
    ids, ranks, gates = lax.fori_loop(0, n_rows, compact, (zeros, zeros, zeros))
    idx_ref[...] = ids.astype(jnp.int32)
    dst_ref[...] = ranks.astype(jnp.int32)
    gate_ref[...] = gates


def _select(aff_rows, cap, n_experts):
    n_rows, l = aff_rows.shape
    tri = jnp.asarray(np.triu(np.ones((l, l), np.float32), k=1), dtype=BF16)
    group = np.arange(n_rows) // n_experts
    same = group[:, None] == group[None, :]
    lower = same & (np.arange(n_rows)[None, :] < np.arange(n_rows)[:, None])
    full = lambda shape: pl.BlockSpec(shape, lambda i: (0, 0))
    slot_major = jax.ShapeDtypeStruct((cap, n_rows), jnp.int32)
    return pl.pallas_call(
        functools.partial(_select_kernel, cap=cap),
        grid=(1,),
        in_specs=[full((n_rows, l)), full((l, l)), full((n_rows, n_rows)), full((n_rows, n_rows))],
        out_specs=[full((cap, n_rows)), full((cap, n_rows)), full((cap, n_rows)), full((n_rows, l))],
        out_shape=[slot_major, slot_major, jax.ShapeDtypeStruct((cap, n_rows), F32),
                   jax.ShapeDtypeStruct((n_rows, l), F32)],
        scratch_shapes=[pltpu.VMEM((n_rows, l), F32), pltpu.VMEM((n_rows, l), F32)],
        compiler_params=_params(1),
    )(aff_rows, tri, jnp.asarray(same, dtype=BF16), jnp.asarray(lower, dtype=BF16))


def _gather_rows_kernel(rows_ref, src_hbm, o_ref, sem):
    n_rows = o_ref.shape[0]
    base = pl.program_id(0) * n_rows

    def start(j, carry):
        pltpu.make_async_copy(src_hbm.at[pl.ds(rows_ref[base + j], 1)], o_ref.at[pl.ds(j, 1)], sem).start()
        return carry

    lax.fori_loop(0, n_rows, start, 0, unroll=DMA_ISSUE_UNROLL)
    pltpu.make_async_copy(src_hbm.at[pl.ds(0, n_rows)], o_ref, sem).wait()


def _gather_rows(flat_rows, src):
    n = flat_rows.shape[0]
    w = src.shape[1]
    r = _tile(n, 1024)
    grid_spec = pltpu.PrefetchScalarGridSpec(
        num_scalar_prefetch=1,
        grid=(n // r,),
        in_specs=[pl.BlockSpec(memory_space=pl.ANY)],
        out_specs=pl.BlockSpec((r, w), lambda i, rows: (i, 0)),
        scratch_shapes=[pltpu.SemaphoreType.DMA],
    )
    return pl.pallas_call(
        _gather_rows_kernel,
        grid_spec=grid_spec,
        out_shape=jax.ShapeDtypeStruct((n, w), src.dtype),
        compiler_params=_params(1),
    )(flat_rows, src)


def _ffn_up_kernel(xp_ref, w1_ref, w3_ref, o_ref, x_s):
    half = xp_ref.shape[1]

    @pl.when(pl.program_id(2) == 0)
    def _():
        lo, hi = _unpack_bf16_pair(xp_ref[...])
        x_s[:, :half] = lo
        x_s[:, half:] = hi

    xin = x_s[...]
    a = _dot(xin, w1_ref[0].astype(BF16))
    o_ref[...] = (a * jax.nn.sigmoid(a) * _dot(xin, w3_ref[0].astype(BF16))).astype(o_ref.dtype)


def _ffn_up(xin_packed, w1, w3):
    n, half = xin_packed.shape
    e, d, ff = w1.shape
    rows_e = n // e
    tm = _tile(rows_e, 1024)
    tn = _tile(ff, 256)
    per_e = rows_e // tm
    return pl.pallas_call(
        _ffn_up_kernel,
        grid=(e, per_e, ff // tn),
        in_specs=[pl.BlockSpec((tm, half), lambda ei, i, j: (ei * per_e + i, 0)),
                  pl.BlockSpec((1, d, tn), lambda ei, i, j: (ei, 0, j)),
                  pl.BlockSpec((1, d, tn), lambda ei, i, j: (ei, 0, j))],
        out_specs=pl.BlockSpec((tm, tn), lambda ei, i, j: (ei * per_e + i, j)),
        out_shape=jax.ShapeDtypeStruct((n, ff), BF16),
        scratch_shapes=[pltpu.VMEM((tm, d), BF16)],
        compiler_params=_params(3),
    )(xin_packed, w1, w3)


def _bf16_bits(x):
    return lax.bitcast_convert_type(x.astype(BF16).astype(F32), jnp.uint32)


def _pack_bf16_pair(lo, hi):
    return (_bf16_bits(lo) >> 16) | _bf16_bits(hi)


def _unpack_bf16_pair(words):
    lo = lax.bitcast_convert_type(words << 16, F32).astype(BF16)
    hi = lax.bitcast_convert_type(words & jnp.uint32(0xFFFF0000), F32).astype(BF16)
    return lo, hi


def _ffn_down_kernel(h_ref, wa_ref, wb_ref, g_ref, o_ref):
    h = h_ref[...]
    gate = g_ref[...]
    o_ref[...] = _pack_bf16_pair(_dot(h, wa_ref[0].astype(BF16)) * gate,
                                 _dot(h, wb_ref[0].astype(BF16)) * gate)


def _ffn_down(hid, w2, gate_col):
    n, ff = hid.shape
    e, _, d = w2.shape
    rows_e = n // e
    half = d // 2
    tw = _tile(half, 256)
    nb = half // tw
    return pl.pallas_call(
        _ffn_down_kernel,
        grid=(e, nb),
        in_specs=[pl.BlockSpec((rows_e, ff), lambda ei, j: (ei, 0)),
                  pl.BlockSpec((1, ff, tw), lambda ei, j: (ei, 0, j)),
                  pl.BlockSpec((1, ff, tw), lambda ei, j: (ei, 0, nb + j)),
                  pl.BlockSpec((rows_e, 1), lambda ei, j: (ei, 0))],
        out_specs=pl.BlockSpec((rows_e, tw), lambda ei, j: (ei, j)),
        out_shape=jax.ShapeDtypeStruct((n, half), jnp.uint32),
        compiler_params=_params(2),
    )(hid, w2, w2, gate_col)


def _row_copy(src, dst_hbm, sem, j, row):
    return pltpu.make_async_copy(src.at[pl.ds(j, 1)], dst_hbm.at[pl.ds(row, 1)], sem)


def _permute_kernel(dest_ref, src_ref, dst_hbm, sem):
    n_rows = src_ref.shape[0]
    base = pl.program_id(0) * n_rows

    def start(j, carry):
        _row_copy(src_ref, dst_hbm, sem, j, dest_ref[base + j]).start()
        return carry

    lax.fori_loop(0, n_rows, start, 0, unroll=DMA_ISSUE_UNROLL)
    pltpu.make_async_copy(src_ref, dst_hbm.at[pl.ds(0, n_rows)], sem).wait()


def _permute_rows(rows, dest):
    n, w = rows.shape
    r = _tile(n, 1024)
    grid_spec = pltpu.PrefetchScalarGridSpec(
        num_scalar_prefetch=1,
        grid=(n // r,),
        in_specs=[pl.BlockSpec((r, w), lambda i, dest_ref: (i, 0))],
        out_specs=pl.BlockSpec(memory_space=pl.ANY),
        scratch_shapes=[pltpu.SemaphoreType.DMA],
    )
    return pl.pallas_call(
        _permute_kernel,
        grid_spec=grid_spec,
        out_shape=jax.ShapeDtypeStruct((n, w), rows.dtype),
        compiler_params=_params(1),
    )(dest, rows)


def _combine_kernel(start_ref, nchunks_ref, first_ref, rows_hbm, lo_ref, hi_ref, x_ref, gt_ref, fg_ref,
                    o_ref, buf, sem, acc_lo, acc_hi, *, n_total):
    step = pl.program_id(0) * pl.num_programs(1) + pl.program_id(1)
    n_steps = pl.num_programs(0) * pl.num_programs(1)
    chunk = buf.shape[1]
    tm = acc_lo.shape[0]
    n_chunks = nchunks_ref[step]
    first = first_ref[step]

    def row0(s, c):
        return pl.multiple_of(jnp.minimum(start_ref[s] + c * chunk, n_total - chunk), SUBLANES)

    def fetch(s, c, slot):
        return pltpu.make_async_copy(rows_hbm.at[pl.ds(row0(s, c), chunk)], buf.at[slot], sem.at[slot])

    @pl.when(step == 0)
    def _():
        fetch(0, 0, 0).start()

    acc_lo[...] = jnp.zeros_like(acc_lo)
    acc_hi[...] = jnp.zeros_like(acc_hi)
    row_lo = lo_ref[0]
    row_hi = hi_ref[0]

    def consume(c, carry):
        slot = (first + c) % 2
        ends_step = c + 1 == n_chunks

        @pl.when(jnp.logical_not(ends_step))
        def _():
            fetch(step, c + 1, 1 - slot).start()

        @pl.when(jnp.logical_and(ends_step, step + 1 < n_steps))
        def _():
            fetch(step + 1, 0, 1 - slot).start()

        fetch(step, c, slot).wait()
        lo, hi = _unpack_bf16_pair(buf[slot])
        row = (row0(step, c) + lax.broadcasted_iota(jnp.int32, (tm, chunk), 1)).astype(F32)
        first_row = jnp.maximum(row_lo, (start_ref[step] + c * chunk).astype(F32))
        onehot = jnp.where(jnp.logical_and(row >= first_row, row < row_hi), 1.0, 0.0).astype(BF16)
        acc_lo[...] += _dot(onehot, lo)
        acc_hi[...] += _dot(onehot, hi)
        return carry

    lax.fori_loop(0, n_chunks, consume, 0)
    y = jnp.concatenate([acc_lo[...], acc_hi[...]], axis=-1)
    x2 = x_ref[0] + gt_ref[0] * y
    ms = jnp.mean(x2 * x2, axis=-1, keepdims=True)
    o_ref[0] = x2 * lax.rsqrt(ms + RMS_EPS) * fg_ref[...]


def _combine(sorted_rows, row_lo, row_hi, x, mod, k_gate, final_g):
    b, l, d = x.shape
    n_total, half = sorted_rows.shape
    tm = _tile(l, 256)
    nt = l // tm
    chunk = COMBINE_CHUNK
    assert n_total >= chunk and n_total % SUBLANES == 0
    tile_lo = row_lo[:, ::tm].astype(jnp.int32).reshape(-1)
    tile_hi = row_hi[:, tm - 1::tm].astype(jnp.int32).reshape(-1)
    start = tile_lo // SUBLANES * SUBLANES
    n_chunks = jnp.maximum(1, (tile_hi - start + chunk - 1) // chunk)
    first = jnp.cumsum(n_chunks) - n_chunks
    col = lambda a: a.reshape(b, l, 1)
    grid_spec = pltpu.PrefetchScalarGridSpec(
        num_scalar_prefetch=3,
        grid=(b, nt),
        in_specs=[pl.BlockSpec(memory_space=pl.ANY),
                  pl.BlockSpec((1, tm, 1), lambda bi, i, *_: (bi, i, 0)),
                  pl.BlockSpec((1, tm, 1), lambda bi, i, *_: (bi, i, 0)),
                  pl.BlockSpec((1, tm, d), lambda bi, i, *_: (bi, i, 0)),
                  pl.BlockSpec((1, 1, d), lambda bi, i, *_: (bi, 0, k_gate)),
                  pl.BlockSpec((1, d), lambda bi, i, *_: (0, 0))],
        out_specs=pl.BlockSpec((1, tm, d), lambda bi, i, *_: (bi, i, 0)),
        scratch_shapes=[pltpu.VMEM((2, chunk, half), jnp.uint32), pltpu.SemaphoreType.DMA((2,)),
                        pltpu.VMEM((tm, half), F32), pltpu.VMEM((tm, half), F32)],
    )
    return pl.pallas_call(
        functools.partial(_combine_kernel, n_total=n_total),
        grid_spec=grid_spec,
        out_shape=jax.ShapeDtypeStruct((b, l, d), F32),
        compiler_params=_params(2),
    )(start, n_chunks.astype(jnp.int32), first.astype(jnp.int32), sorted_rows, col(row_lo), col(row_hi),
      x, mod, final_g.reshape(1, d))


def kernel(x, c, ctx, c_ctx, w_mod, b_mod, norm_mix_g, w_in, b_gate, w_fourier, na_rel_bias,
           w_na_out, w_out, norm_ffn_g, w_router, w1, w3, w2, final_norm_g):
    assert w_mod.shape[0] == 1, "single-layer stack only"
    b, l, d = x.shape
    ctx_len = ctx.shape[1]
    f_width = w_fourier.shape[1]
    na_width = w_na_out.shape[1]
    n_experts = w_router.shape[-1]
    in_width = w_in.shape[-1]
    off_q = f_width
    off_k = off_q + na_width
    off_g = off_k + 2 * na_width
    cap = EC_FACTOR * l // n_experts
    SH_M, SC_M, GT_M, SH_F, SC_F, GT_F = range(N_MOD)

    pad = (-(b + 1)) % 16
    cond = jnp.concatenate([c, c_ctx[None], jnp.zeros((pad, d), F32)], axis=0)
    mod_all = _ada_params(cond, w_mod[0], b_mod[0])
    mod = mod_all[:b].reshape(b, 1, N_MOD * d)
    mod_ctx = mod_all[b:b + 1].reshape(1, 1, N_MOD * d)

    w_in_bf = w_in[0].astype(BF16)
    xn = _norm_mod(x, norm_mix_g[0], mod, SH_M, SC_M)
    cn = _norm_mod(ctx, norm_mix_g[0], mod_ctx, SH_M, SC_M)
    b_full = jnp.concatenate([jnp.zeros((off_g,), F32), b_gate[0]]).reshape(1, in_width)
    z = _proj_in(xn.reshape(b * l, d), w_in_bf, b_full, off_g)
    zc = _matmul_cols(cn.reshape(b * ctx_len, d), w_in_bf, off_k, 2 * na_width)
    z3 = z.reshape(b, l, in_width)

    ab = _fourier_chan(z3, f_width)
    y_four = _fourier_pos(ab.reshape(b, 2 * l, f_width))
    bias_slabs = _na_bias_slabs(na_rel_bias[0], l // GRID_W)
    o_na = _neighbourhood_attention(z3, zc.reshape(b, ctx_len, 2 * na_width), bias_slabs, off_q, na_width)
    mixed = _merge(y_four.reshape(b * l, f_width), o_na.reshape(b * l, na_width),
                   w_fourier[0], w_na_out[0], z, off_g)
    x1 = _out_proj_residual(mixed, w_out[0], x.reshape(b * l, d), mod, GT_M, l)

    x1_3 = x1.reshape(b, l, d)
    aff_t, xn_packed = _router(x1_3, norm_ffn_g[0], mod, SH_F, SC_F, w_router[0].T)
    idx_t, dst_t, gate_t, cum = _select(aff_t.reshape(b * n_experts, l), cap, n_experts)

    def expert_major(slot_major, batch_stride):
        v = slot_major.T.reshape(b, n_experts, cap)
        offs = (jnp.arange(b, dtype=jnp.int32) * batch_stride)[None, :, None]
        return (jnp.transpose(v, (1, 0, 2)) + offs.astype(v.dtype)).reshape(-1)

    per_batch = n_experts * cap
    xin = _gather_rows(expert_major(idx_t, l), xn_packed.reshape(b * l, d // 2))
    hid = _ffn_up(xin, w1[0], w3[0])
    out = _ffn_down(hid, w2[0], expert_major(gate_t, 0).reshape(-1, 1))
    sorted_rows = _permute_rows(out, expert_major(dst_t, per_batch))
    before = cum.reshape(b, n_experts, l)[:, 0, :] + (jnp.arange(b, dtype=F32) * per_batch)[:, None]
    after = jnp.concatenate([before[:, 1:], before[:, :1] + per_batch], axis=1)
    return _combine(sorted_rows, before, after, x1_3, mod, GT_F, final_norm_g)
```

```python
import functools

import numpy as np
import jax
import jax.numpy as jnp
from jax import lax
from jax.experimental import pallas as pl
from jax.experimental.pallas import tpu as pltpu

GRID_W = 64
F_GROUPS = 4
NA_HEAD_DIM = 128
NA_ROWS_MAX = 8
NA_COLS = 16
EC_FACTOR = 2
N_MOD = 6
RMS_EPS = 1e-6

SUBLANES = 8
COMBINE_CHUNK = 256
VMEM_LIMIT_BYTES = 56 * 1024 * 1024
BISECT_STEPS = 160
NA_ROWS_PER_STEP = 32
NA_HEADS_PER_STEP = 4
DMA_ISSUE_UNROLL = 32

F32 = jnp.float32
BF16 = jnp.bfloat16
NT_DIMS = (((1,), (1,)), ((), ()))


def _params(n_axes):
    return pltpu.CompilerParams(
        dimension_semantics=("arbitrary",) * n_axes, vmem_limit_bytes=VMEM_LIMIT_BYTES)


def _tile(dim, pref):
    t = min(dim, pref)
    while dim % t:
        t -= 1
    return t


def _dot(a, b):
    return jnp.dot(a, b, preferred_element_type=F32)


def _rms_modulate(x, gain, scale, shift):
    ms = jnp.mean(x * x, axis=-1, keepdims=True)
    y = x * lax.rsqrt(ms + RMS_EPS) * gain
    return y * (1.0 + scale) + shift


def _ada_kernel(c_ref, w_ref, b_ref, o_ref):
    cond = c_ref[...]
    act = cond * jax.nn.sigmoid(cond)
    o_ref[...] = _dot(act.astype(BF16), w_ref[...].astype(BF16)) + b_ref[...]


def _ada_params(cond, w_mod, b_mod):
    m, d = cond.shape
    n = w_mod.shape[1]
    tn = _tile(n, 512)
    return pl.pallas_call(
        _ada_kernel,
        grid=(n // tn,),
        in_specs=[pl.BlockSpec((m, d), lambda j: (0, 0)),
                  pl.BlockSpec((d, tn), lambda j: (0, j)),
                  pl.BlockSpec((1, tn), lambda j: (0, j))],
        out_specs=pl.BlockSpec((m, tn), lambda j: (0, j)),
        out_shape=jax.ShapeDtypeStruct((m, n), F32),
        compiler_params=_params(1),
    )(cond, w_mod, b_mod.reshape(1, n))


def _norm_mod_kernel(x_ref, g_ref, sc_ref, sh_ref, o_ref):
    o_ref[0] = _rms_modulate(x_ref[0], g_ref[...], sc_ref[0], sh_ref[0]).astype(o_ref.dtype)


def _norm_mod(x, gain, mod, k_shift, k_scale):
    b, l, d = x.shape
    per_batch = mod.shape[0] == b
    tm = _tile(l, 512)
    bsel = (lambda i: i) if per_batch else (lambda i: 0)
    return pl.pallas_call(
        _norm_mod_kernel,
        grid=(b, l // tm),
        in_specs=[pl.BlockSpec((1, tm, d), lambda bi, i: (bi, i, 0)),
                  pl.BlockSpec((1, d), lambda bi, i: (0, 0)),
                  pl.BlockSpec((1, 1, d), lambda bi, i: (bsel(bi), 0, k_scale)),
                  pl.BlockSpec((1, 1, d), lambda bi, i: (bsel(bi), 0, k_shift))],
        out_specs=pl.BlockSpec((1, tm, d), lambda bi, i: (bi, i, 0)),
        out_shape=jax.ShapeDtypeStruct((b, l, d), BF16),
        compiler_params=_params(2),
    )(x, gain.reshape(1, d), mod, mod)


def _proj_in_kernel(a_ref, w_ref, b_ref, o_ref, *, gate_block0):
    acc = _dot(a_ref[...], w_ref[...])
    j = pl.program_id(1)

    @pl.when(j < gate_block0)
    def _():
        o_ref[...] = acc.astype(o_ref.dtype)

    @pl.when(j >= gate_block0)
    def _():
        o_ref[...] = jax.nn.sigmoid(acc + b_ref[...]).astype(o_ref.dtype)


def _proj_in(a, w, b_full, off_gate):
    m, k = a.shape
    n = w.shape[1]
    tm = _tile(m, 1024)
    tn = _tile(np.gcd(n, off_gate), 1024)
    return pl.pallas_call(
        functools.partial(_proj_in_kernel, gate_block0=off_gate // tn),
        grid=(m // tm, n // tn),
        in_specs=[pl.BlockSpec((tm, k), lambda i, j: (i, 0)),
                  pl.BlockSpec((k, tn), lambda i, j: (0, j)),
                  pl.BlockSpec((1, tn), lambda i, j: (0, j))],
        out_specs=pl.BlockSpec((tm, tn), lambda i, j: (i, j)),
        out_shape=jax.ShapeDtypeStruct((m, n), BF16),
        compiler_params=_params(2),
    )(a, w, b_full)


def _mm_kernel(a_ref, w_ref, o_ref):
    o_ref[...] = _dot(a_ref[...], w_ref[...]).astype(o_ref.dtype)


def _matmul_cols(a, w, col0, ncols):
    m, k = a.shape
    tm = _tile(m, 1024)
    tn = _tile(np.gcd(ncols, col0) if col0 else ncols, 512)
    j0 = col0 // tn
    return pl.pallas_call(
        _mm_kernel,
        grid=(m // tm, ncols // tn),
        in_specs=[pl.BlockSpec((tm, k), lambda i, j: (i, 0)),
                  pl.BlockSpec((k, tn), lambda i, j: (0, j0 + j))],
        out_specs=pl.BlockSpec((tm, tn), lambda i, j: (i, j)),
        out_shape=jax.ShapeDtypeStruct((m, ncols), BF16),
        compiler_params=_params(2),
    )(a, w)


def _dft_tables(n, scale):
    k = np.arange(n)
    ang = 2.0 * np.pi * ((k[:, None] * k[None, :]) % n) / n
    return np.cos(ang) * scale, np.sin(ang) * scale


def _fourier_chan_kernel(u_ref, t_ref, o_ref):
    dg = u_ref.shape[-1]
    res = _dot(u_ref[0], t_ref[...])
    o_ref[0, 0] = res[:, :dg].astype(o_ref.dtype)
    o_ref[0, 1] = res[:, dg:].astype(o_ref.dtype)


def _fourier_chan(z3, f_width):
    b, l, _ = z3.shape
    dg = f_width // F_GROUPS
    cd, sd = _dft_tables(dg, dg ** -0.5)
    table = jnp.asarray(np.concatenate([cd, sd], axis=1), dtype=BF16)
    tm = _tile(l, 1024)
    return pl.pallas_call(
        _fourier_chan_kernel,
        grid=(b, l // tm, F_GROUPS),
        in_specs=[pl.BlockSpec((1, tm, dg), lambda bi, i, g: (bi, i, g)),
                  pl.BlockSpec((dg, 2 * dg), lambda bi, i, g: (0, 0))],
        out_specs=pl.BlockSpec((1, 2, tm, dg), lambda bi, i, g: (bi, 0, i, g)),
        out_shape=jax.ShapeDtypeStruct((b, 2, l, f_width), BF16),
        compiler_params=_params(3),
    )(z3, table)


def _fourier_pos_kernel(t_ref, ab_ref, o_ref):
    o_ref[0] = _dot(t_ref[...], ab_ref[0]).astype(o_ref.dtype)


def _fourier_pos(ab):
    b, l2, fw = ab.shape
    l = l2 // 2
    cl, sl = _dft_tables(l, l ** -0.5)
    table = jnp.asarray(np.concatenate([cl, -sl], axis=1), dtype=BF16)
    tm = _tile(l, 1024)
    tn = _tile(fw, 512)
    return pl.pallas_call(
        _fourier_pos_kernel,
        grid=(b, fw // tn, l // tm),
        in_specs=[pl.BlockSpec((tm, l2), lambda bi, j, i: (i, 0)),
                  pl.BlockSpec((1, l2, tn), lambda bi, j, i: (bi, 0, j))],
        out_specs=pl.BlockSpec((1, tm, tn), lambda bi, j, i: (bi, i, j)),
        out_shape=jax.ShapeDtypeStruct((b, l, fw), BF16),
        compiler_params=_params(3),
    )(table, ab)


def _na_geometry(rows):
    kr = min(NA_ROWS_MAX, rows)
    starts = [int(np.clip(r - kr // 2, 0, rows - kr)) - r + NA_ROWS_MAX - 1 for r in range(rows)]
    return kr, min(starts), max(starts) - min(starts) + 1


def _na_bias_slabs(rel_bias, rows):
    kr, min_start, n_slabs = _na_geometry(rows)
    cols = np.arange(GRID_W)
    win_start = np.clip(cols - NA_COLS // 2, 0, GRID_W - NA_COLS)
    rel_col = cols[None, :] - win_start[:, None]
    col_mask = (rel_col >= 0) & (rel_col < NA_COLS)
    dc_idx = np.clip(cols[None, :] - cols[:, None] + NA_COLS - 1, 0, 2 * NA_COLS - 2)
    neg = jnp.finfo(F32).min
    bias_c = jnp.where(col_mask[None, None], rel_bias.astype(F32)[:, :, dc_idx], neg)
    slabs = jnp.stack([bias_c[:, min_start + s:min_start + s + kr] for s in range(n_slabs)], axis=1)
    slabs = jnp.transpose(slabs, (0, 1, 3, 2, 4))
    h = rel_bias.shape[0]
    return slabs.reshape(h, n_slabs, GRID_W, kr * GRID_W)


def _na_kernel(q_ref, k_ref, v_ref, kc_ref, vc_ref, bias_ref, o_ref, s_ctx_all, p_ctx_all, o_loc_all,
               *, heads_per_step, **geometry):
    for hh in range(heads_per_step):
        cols = slice(hh * NA_HEAD_DIM, (hh + 1) * NA_HEAD_DIM)
        _na_one_head(q_ref.at[0, :, cols], k_ref.at[0, :, cols], v_ref.at[0, :, cols], kc_ref.at[0, :, cols],
                     vc_ref.at[0, :, cols], bias_ref.at[hh], o_ref.at[0, :, cols],
                     s_ctx_all, p_ctx_all, o_loc_all, **geometry)


def _na_one_head(q_ref, k_ref, v_ref, kc_ref, vc_ref, bias_ref, o_ref, s_ctx_all, p_ctx_all, o_loc_all,
                 *, rows, rows_per_step, kr, min_start, scale):
    s_ctx_all[...] = lax.dot_general(q_ref[...], kc_ref[...], NT_DIMS, preferred_element_type=F32) * scale

    def scores(r):
        rs = jnp.clip(r - kr // 2, 0, rows - kr)
        slab = rs - r + (NA_ROWS_MAX - 1 - min_start)
        q0 = pl.multiple_of(r * GRID_W, GRID_W)
        k0 = pl.multiple_of(rs * GRID_W, GRID_W)
        qr = q_ref[pl.ds(q0, GRID_W), :]
        kw = k_ref[pl.ds(k0, kr * GRID_W), :]
        s_loc = lax.dot_general(qr, kw, NT_DIMS, preferred_element_type=F32) * scale + bias_ref[slab]
        return q0, k0, s_loc

    def softmax(q0, s_loc):
        s_ctx = s_ctx_all[pl.ds(q0, GRID_W), :]
        m = jnp.maximum(jnp.max(s_loc, axis=-1, keepdims=True), jnp.max(s_ctx, axis=-1, keepdims=True))
        p_loc = jnp.exp(s_loc - m)
        p_ctx = jnp.exp(s_ctx - m)
        inv = 1.0 / (jnp.sum(p_loc, axis=-1, keepdims=True) + jnp.sum(p_ctx, axis=-1, keepdims=True))
        return (p_loc * inv).astype(BF16), (p_ctx * inv).astype(BF16)

    def row_group(g, carry):
        sc = [scores(g * rows_per_step + u) for u in range(rows_per_step)]
        pr = [softmax(q0, s_loc) for q0, _, s_loc in sc]
        for (q0, k0, _), (p_loc, p_ctx) in zip(sc, pr):
            p_ctx_all[pl.ds(q0, GRID_W), :] = p_ctx
            o_loc_all[pl.ds(q0, GRID_W), :] = _dot(p_loc, v_ref[pl.ds(k0, kr * GRID_W), :])
        return carry

    lax.fori_loop(0, rows // rows_per_step, row_group, 0)
    o_ref[...] = (o_loc_all[...] + _dot(p_ctx_all[...], vc_ref[...])).astype(o_ref.dtype)


def _neighbourhood_attention(z3, zc3, bias_slabs, off_q, na_width):
    b, l, _ = z3.shape
    ctx_len = zc3.shape[1]
    dh = NA_HEAD_DIM
    heads = na_width // dh
    rows = l // GRID_W
    kr, min_start, n_slabs = _na_geometry(rows)
    hp = _tile(heads, NA_HEADS_PER_STEP)
    w = hp * dh
    qb, kb, vb = off_q // w, (off_q + na_width) // w, (off_q + 2 * na_width) // w
    kern = functools.partial(_na_kernel, heads_per_step=hp, rows=rows,
                             rows_per_step=_tile(rows, NA_ROWS_PER_STEP), kr=kr,
                             min_start=min_start, scale=dh ** -0.5)
    return pl.pallas_call(
        kern,
        grid=(heads // hp, b),
        in_specs=[pl.BlockSpec((1, l, w), lambda h, bi: (bi, 0, qb + h)),
                  pl.BlockSpec((1, l, w), lambda h, bi: (bi, 0, kb + h)),
                  pl.BlockSpec((1, l, w), lambda h, bi: (bi, 0, vb + h)),
                  pl.BlockSpec((1, ctx_len, w), lambda h, bi: (bi, 0, h)),
                  pl.BlockSpec((1, ctx_len, w), lambda h, bi: (bi, 0, heads // hp + h)),
                  pl.BlockSpec((hp, n_slabs, GRID_W, kr * GRID_W), lambda h, bi: (h, 0, 0, 0))],
        out_specs=pl.BlockSpec((1, l, w), lambda h, bi: (bi, 0, h)),
        out_shape=jax.ShapeDtypeStruct((b, l, na_width), BF16),
        scratch_shapes=[pltpu.VMEM((l, ctx_len), F32), pltpu.VMEM((l, ctx_len), BF16),
                        pltpu.VMEM((l, dh), F32)],
        compiler_params=_params(2),
    )(z3, z3, z3, zc3, zc3, bias_slabs)


def _merge_kernel(yf_ref, on_ref, wf_ref, wn_ref, gf_ref, gn_ref, o_ref, wf_s, wn_s):
    @pl.when(pl.program_id(1) == 0)
    def _():
        wf_s[...] = wf_ref[...].astype(BF16)
        wn_s[...] = wn_ref[...].astype(BF16)

    y_f = _dot(yf_ref[...], wf_s[...])
    y_n = _dot(on_ref[...], wn_s[...])
    o_ref[...] = (gf_ref[...].astype(F32) * y_f + gn_ref[...].astype(F32) * y_n).astype(o_ref.dtype)


def _merge(y_four, o_na, w_f, w_n, z, off_gate):
    m, kf = y_four.shape
    kn = o_na.shape[1]
    d = w_f.shape[1]
    tm = _tile(m, 1024)
    tn = _tile(np.gcd(d, off_gate), 512)
    gf0, gn0 = off_gate // tn, (off_gate + d) // tn
    return pl.pallas_call(
        _merge_kernel,
        grid=(d // tn, m // tm),
        in_specs=[pl.BlockSpec((tm, kf), lambda j, i: (i, 0)),
                  pl.BlockSpec((tm, kn), lambda j, i: (i, 0)),
                  pl.BlockSpec((kf, tn), lambda j, i: (0, j)),
                  pl.BlockSpec((kn, tn), lambda j, i: (0, j)),
                  pl.BlockSpec((tm, tn), lambda j, i: (i, gf0 + j)),
                  pl.BlockSpec((tm, tn), lambda j, i: (i, gn0 + j))],
        out_specs=pl.BlockSpec((tm, tn), lambda j, i: (i, j)),
        out_shape=jax.ShapeDtypeStruct((m, d), BF16),
        scratch_shapes=[pltpu.VMEM((kf, tn), BF16), pltpu.VMEM((kn, tn), BF16)],
        compiler_params=_params(2),
    )(y_four, o_na, w_f, w_n, z, z)


def _out_proj_kernel(a_ref, w_ref, x_ref, gt_ref, o_ref, w_s):
    @pl.when(pl.program_id(1) == 0)
    def _():
        w_s[...] = w_ref[...].astype(BF16)

    o_ref[...] = x_ref[...] + gt_ref[0] * _dot(a_ref[...], w_s[...])


def _out_proj_residual(a, w, x2d, mod, k_gate, seq):
    m, k = a.shape
    d = w.shape[1]
    tm = _tile(seq, 1024)
    tn = _tile(d, 512)
    gate0 = k_gate * (d // tn)
    return pl.pallas_call(
        _out_proj_kernel,
        grid=(d // tn, m // tm),
        in_specs=[pl.BlockSpec((tm, k), lambda j, i: (i, 0)),
                  pl.BlockSpec((k, tn), lambda j, i: (0, j)),
                  pl.BlockSpec((tm, tn), lambda j, i: (i, j)),
                  pl.BlockSpec((1, 1, tn), lambda j, i: (i * tm // seq, 0, gate0 + j))],
        out_specs=pl.BlockSpec((tm, tn), lambda j, i: (i, j)),
        out_shape=jax.ShapeDtypeStruct((m, d), F32),
        scratch_shapes=[pltpu.VMEM((k, tn), BF16)],
        compiler_params=_params(2),
    )(a, w, x2d, mod)


def _router_kernel(x_ref, g_ref, sc_ref, sh_ref, wr_ref, aff_ref, xp_ref):
    xn = _rms_modulate(x_ref[0], g_ref[...], sc_ref[0], sh_ref[0])
    half = xn.shape[1] // 2
    xp_ref[0] = _pack_bf16_pair(xn[:, :half], xn[:, half:])
    x_hi = xn.astype(BF16)
    x_lo = (xn - x_hi.astype(F32)).astype(BF16)
    w = wr_ref[...]
    w_hi = w.astype(BF16)
    w_lo = (w - w_hi.astype(F32)).astype(BF16)
    logits = (lax.dot_general(w_hi, x_hi, NT_DIMS, preferred_element_type=F32)
              + lax.dot_general(w_lo, x_hi, NT_DIMS, preferred_element_type=F32)
              + lax.dot_general(w_hi, x_lo, NT_DIMS, preferred_element_type=F32))
    ex = jnp.exp(logits - jnp.max(logits, axis=0, keepdims=True))
    aff_ref[0] = ex / jnp.sum(ex, axis=0, keepdims=True)


def _router(x, gain, mod, k_shift, k_scale, w_router_t):
    b, l, d = x.shape
    e = w_router_t.shape[0]
    tm = _tile(l, 256)
    return pl.pallas_call(
        _router_kernel,
        grid=(b, l // tm),
        in_specs=[pl.BlockSpec((1, tm, d), lambda bi, i: (bi, i, 0)),
                  pl.BlockSpec((1, d), lambda bi, i: (0, 0)),
                  pl.BlockSpec((1, 1, d), lambda bi, i: (bi, 0, k_scale)),
                  pl.BlockSpec((1, 1, d), lambda bi, i: (bi, 0, k_shift)),
                  pl.BlockSpec((e, d), lambda bi, i: (0, 0))],
        out_specs=[pl.BlockSpec((1, e, tm), lambda bi, i: (bi, 0, i)),
                   pl.BlockSpec((1, tm, d // 2), lambda bi, i: (bi, i, 0))],
        out_shape=[jax.ShapeDtypeStruct((b, e, l), F32), jax.ShapeDtypeStruct((b, l, d // 2), jnp.uint32)],
        compiler_params=_params(2),
    )(x, gain.reshape(1, d), mod, mod, w_router_t)


def _select_kernel(a_ref, tri_ref, same_ref, lower_ref, idx_ref, dst_ref, gate_ref, cum_ref, pos_ref, rank_ref,
                   *, cap):
    a = a_ref[...]
    n_rows, l = a.shape
    capf = float(cap)

    def count(mask):
        return jnp.sum(jnp.where(mask, 1.0, 0.0), axis=-1, keepdims=True)

    def halve(_, lo_hi):
        lo, hi = lo_hi
        mid = 0.5 * lo + 0.5 * hi
        ok = count(a_ref[...] >= mid) >= capf
        return jnp.where(ok, mid, lo), jnp.where(ok, hi, mid)

    _, hi = lax.fori_loop(0, BISECT_STEPS, halve,
                          (jnp.zeros((n_rows, 1), F32), jnp.full((n_rows, 1), 2.0, F32)))
    tau = jnp.max(jnp.where(a < hi, a, -1.0), axis=-1, keepdims=True)
    above = a > tau
    tied = a == tau
    need = capf - count(above)
    tri = tri_ref[...]
    tied_before = _dot(jnp.where(tied, 1.0, 0.0).astype(BF16), tri)
    chosen = above | (tied & (tied_before < need))
    chosen_b = jnp.where(chosen, 1.0, 0.0).astype(BF16)
    pos = _dot(chosen_b, tri)
    pos_ref[...] = jnp.where(chosen, pos, -1.0)
    per_token = _dot(same_ref[...], chosen_b)
    before_token = _dot(per_token.astype(BF16), tri)
    cum_ref[...] = before_token
    rank_ref[...] = before_token + _dot(lower_ref[...], chosen_b)

    tok = lax.broadcasted_iota(jnp.int32, (cap, l), 1).astype(F32)
    slot = lax.broadcasted_iota(jnp.int32, (cap, l), 0).astype(F32)
    lane = lax.broadcasted_iota(jnp.int32, (cap, n_rows), 1)

    def compact(r, accs):
        hit = pos_ref[pl.ds(r, 1), :] == slot
        picked = [jnp.sum(jnp.where(hit, v, 0.0), axis=-1, keepdims=True)
                  for v in (tok, rank_ref[pl.ds(r, 1), :], a_ref[pl.ds(r, 1), :])]
        return tuple(jnp.where(lane == r, p, acc) for p, acc in zip(picked, accs))

    zeros = jnp.zeros((cap, n_rows), F32)
    ids, ranks, gates = lax.fori_loop(0, n_rows, compact, (zeros, zeros, zeros))
    idx_ref[...] = ids.astype(jnp.int32)
    dst_ref[...] = ranks.astype(jnp.int32)
    gate_ref[...] = gates


def _select(aff_rows, cap, n_experts):
    n_rows, l = aff_rows.shape
    tri = jnp.asarray(np.triu(np.ones((l, l), np.float32), k=1), dtype=BF16)
    group = np.arange(n_rows) // n_experts
    same = group[:, None] == group[None, :]
    lower = same & (np.arange(n_rows)[None, :] < np.arange(n_rows)[:, None])
    full = lambda shape: pl.BlockSpec(shape, lambda i: (0, 0))
    slot_major = jax.ShapeDtypeStruct((cap, n_rows), jnp.int32)
    return pl.pallas_call(
        functools.partial(_select_kernel, cap=cap),
        grid=(1,),
        in_specs=[full((n_rows, l)), full((l, l)), full((n_rows, n_rows)), full((n_rows, n_rows))],
        out_specs=[full((cap, n_rows)), full((cap, n_rows)), full((cap, n_rows)), full((n_rows, l))],
        out_shape=[slot_major, slot_major, jax.ShapeDtypeStruct((cap, n_rows), F32),
                   jax.ShapeDtypeStruct((n_rows, l), F32)],
        scratch_shapes=[pltpu.VMEM((n_rows, l), F32), pltpu.VMEM((n_rows, l), F32)],
        compiler_params=_params(1),
    )(aff_rows, tri, jnp.asarray(same, dtype=BF16), jnp.asarray(lower, dtype=BF16))


def _gather_rows_kernel(rows_ref, src_hbm, o_ref, sem):
    n_rows = o_ref.shape[0]
    base = pl.program_id(0) * n_rows

    def start(j, carry):
        pltpu.make_async_copy(src_hbm.at[pl.ds(rows_ref[base + j], 1)], o_ref.at[pl.ds(j, 1)], sem).start()
        return carry

    lax.fori_loop(0, n_rows, start, 0, unroll=DMA_ISSUE_UNROLL)
    pltpu.make_async_copy(src_hbm.at[pl.ds(0, n_rows)], o_ref, sem).wait()


def _gather_rows(flat_rows, src):
    n = flat_rows.shape[0]
    w = src.shape[1]
    r = _tile(n, 1024)
    grid_spec = pltpu.PrefetchScalarGridSpec(
        num_scalar_prefetch=1,
        grid=(n // r,),
        in_specs=[pl.BlockSpec(memory_space=pl.ANY)],
        out_specs=pl.BlockSpec((r, w), lambda i, rows: (i, 0)),
        scratch_shapes=[pltpu.SemaphoreType.DMA],
    )
    return pl.pallas_call(
        _gather_rows_kernel,
        grid_spec=grid_spec,
        out_shape=jax.ShapeDtypeStruct((n, w), src.dtype),
        compiler_params=_params(1),
    )(flat_rows, src)


def _ffn_up_kernel(xp_ref, w1_ref, w3_ref, o_ref, x_s):
    half = xp_ref.shape[1]

    @pl.when(pl.program_id(2) == 0)
    def _():
        lo, hi = _unpack_bf16_pair(xp_ref[...])
        x_s[:, :half] = lo
        x_s[:, half:] = hi

    xin = x_s[...]
    a = _dot(xin, w1_ref[0].astype(BF16))
    o_ref[...] = (a * jax.nn.sigmoid(a) * _dot(xin, w3_ref[0].astype(BF16))).astype(o_ref.dtype)


def _ffn_up(xin_packed, w1, w3):
    n, half = xin_packed.shape
    e, d, ff = w1.shape
    rows_e = n // e
    tm = _tile(rows_e, 1024)
    tn = _tile(ff, 256)
    per_e = rows_e // tm
    return pl.pallas_call(
        _ffn_up_kernel,
        grid=(e, per_e, ff // tn),
        in_specs=[pl.BlockSpec((tm, half), lambda ei, i, j: (ei * per_e + i, 0)),
                  pl.BlockSpec((1, d, tn), lambda ei, i, j: (ei, 0, j)),
                  pl.BlockSpec((1, d, tn), lambda ei, i, j: (ei, 0, j))],
        out_specs=pl.BlockSpec((tm, tn), lambda ei, i, j: (ei * per_e + i, j)),
        out_shape=jax.ShapeDtypeStruct((n, ff), BF16),
        scratch_shapes=[pltpu.VMEM((tm, d), BF16)],
        compiler_params=_params(3),
    )(xin_packed, w1, w3)


def _bf16_bits(x):
    return lax.bitcast_convert_type(x.astype(BF16).astype(F32), jnp.uint32)


def _pack_bf16_pair(lo, hi):
    return (_bf16_bits(lo) >> 16) | _bf16_bits(hi)


def _unpack_bf16_pair(words):
    lo = lax.bitcast_convert_type(words << 16, F32).astype(BF16)
    hi = lax.bitcast_convert_type(words & jnp.uint32(0xFFFF0000), F32).astype(BF16)
    return lo, hi


def _ffn_down_kernel(h_ref, wa_ref, wb_ref, g_ref, o_ref):
    h = h_ref[...]
    gate = g_ref[...]
    o_ref[...] = _pack_bf16_pair(_dot(h, wa_ref[0].astype(BF16)) * gate,
                                 _dot(h, wb_ref[0].astype(BF16)) * gate)


def _ffn_down(hid, w2, gate_col):
    n, ff = hid.shape
    e, _, d = w2.shape
    rows_e = n // e
    half = d // 2
    tw = _tile(half, 256)
    nb = half // tw
    return pl.pallas_call(
        _ffn_down_kernel,
        grid=(e, nb),
        in_specs=[pl.BlockSpec((rows_e, ff), lambda ei, j: (ei, 0)),
                  pl.BlockSpec((1, ff, tw), lambda ei, j: (ei, 0, j)),
                  pl.BlockSpec((1, ff, tw), lambda ei, j: (ei, 0, nb + j)),
                  pl.BlockSpec((rows_e, 1), lambda ei, j: (ei, 0))],
        out_specs=pl.BlockSpec((rows_e, tw), lambda ei, j: (ei, j)),
        out_shape=jax.ShapeDtypeStruct((n, half), jnp.uint32),
        compiler_params=_params(2),
    )(hid, w2, w2, gate_col)


def _row_copy(src, dst_hbm, sem, j, row):
    return pltpu.make_async_copy(src.at[pl.ds(j, 1)], dst_hbm.at[pl.ds(row, 1)], sem)


def _permute_kernel(dest_ref, src_ref, dst_hbm, sem):
    n_rows = src_ref.shape[0]
    base = pl.program_id(0) * n_rows

    def start(j, carry):
        _row_copy(src_ref, dst_hbm, sem, j, dest_ref[base + j]).start()
        return carry

    lax.fori_loop(0, n_rows, start, 0, unroll=DMA_ISSUE_UNROLL)
    pltpu.make_async_copy(src_ref, dst_hbm.at[pl.ds(0, n_rows)], sem).wait()


def _permute_rows(rows, dest):
    n, w = rows.shape
    r = _tile(n, 1024)
    grid_spec = pltpu.PrefetchScalarGridSpec(
        num_scalar_prefetch=1,
        grid=(n // r,),
        in_specs=[pl.BlockSpec((r, w), lambda i, dest_ref: (i, 0))],
        out_specs=pl.BlockSpec(memory_space=pl.ANY),
        scratch_shapes=[pltpu.SemaphoreType.DMA],
    )
    return pl.pallas_call(
        _permute_kernel,
        grid_spec=grid_spec,
        out_shape=jax.ShapeDtypeStruct((n, w), rows.dtype),
        compiler_params=_params(1),
    )(dest, rows)


def _combine_kernel(start_ref, nchunks_ref, first_ref, rows_hbm, lo_ref, hi_ref, x_ref, gt_ref, fg_ref,
                    o_ref, buf, sem, acc_lo, acc_hi, *, n_total):
    step = pl.program_id(0) * pl.num_programs(1) + pl.program_id(1)
    n_steps = pl.num_programs(0) * pl.num_programs(1)
    chunk = buf.shape[1]
    tm = acc_lo.shape[0]
    n_chunks = nchunks_ref[step]
    first = first_ref[step]

    def row0(s, c):
        return pl.multiple_of(jnp.minimum(start_ref[s] + c * chunk, n_total - chunk), SUBLANES)

    def fetch(s, c, slot):
        return pltpu.make_async_copy(rows_hbm.at[pl.ds(row0(s, c), chunk)], buf.at[slot], sem.at[slot])

    @pl.when(step == 0)
    def _():
        fetch(0, 0, 0).start()

    acc_lo[...] = jnp.zeros_like(acc_lo)
    acc_hi[...] = jnp.zeros_like(acc_hi)
    row_lo = lo_ref[0]
    row_hi = hi_ref[0]

    def consume(c, carry):
        slot = (first + c) % 2
        ends_step = c + 1 == n_chunks

        @pl.when(jnp.logical_not(ends_step))
        def _():
            fetch(step, c + 1, 1 - slot).start()

        @pl.when(jnp.logical_and(ends_step, step + 1 < n_steps))
        def _():
            fetch(step + 1, 0, 1 - slot).start()

        fetch(step, c, slot).wait()
        lo, hi = _unpack_bf16_pair(buf[slot])
        row = (row0(step, c) + lax.broadcasted_iota(jnp.int32, (tm, chunk), 1)).astype(F32)
        first_row = jnp.maximum(row_lo, (start_ref[step] + c * chunk).astype(F32))
        onehot = jnp.where(jnp.logical_and(row >= first_row, row < row_hi), 1.0, 0.0).astype(BF16)
        acc_lo[...] += _dot(onehot, lo)
        acc_hi[...] += _dot(onehot, hi)
        return carry

    lax.fori_loop(0, n_chunks, consume, 0)
    y = jnp.concatenate([acc_lo[...], acc_hi[...]], axis=-1)
    x2 = x_ref[0] + gt_ref[0] * y
    ms = jnp.mean(x2 * x2, axis=-1, keepdims=True)
    o_ref[0] = x2 * lax.rsqrt(ms + RMS_EPS) * fg_ref[...]


def _combine(sorted_rows, row_lo, row_hi, x, mod, k_gate, final_g):
    b, l, d = x.shape
    n_total, half = sorted_rows.shape
    tm = _tile(l, 256)
    nt = l // tm
    chunk = COMBINE_CHUNK
    assert n_total >= chunk and n_total % SUBLANES == 0
    tile_lo = row_lo[:, ::tm].astype(jnp.int32).reshape(-1)
    tile_hi = row_hi[:, tm - 1::tm].astype(jnp.int32).reshape(-1)
    start = tile_lo // SUBLANES * SUBLANES
    n_chunks = jnp.maximum(1, (tile_hi - start + chunk - 1) // chunk)
    first = jnp.cumsum(n_chunks) - n_chunks
    col = lambda a: a.reshape(b, l, 1)
    grid_spec = pltpu.PrefetchScalarGridSpec(
        num_scalar_prefetch=3,
        grid=(b, nt),
        in_specs=[pl.BlockSpec(memory_space=pl.ANY),
                  pl.BlockSpec((1, tm, 1), lambda bi, i, *_: (bi, i, 0)),
                  pl.BlockSpec((1, tm, 1), lambda bi, i, *_: (bi, i, 0)),
                  pl.BlockSpec((1, tm, d), lambda bi, i, *_: (bi, i, 0)),
                  pl.BlockSpec((1, 1, d), lambda bi, i, *_: (bi, 0, k_gate)),
                  pl.BlockSpec((1, d), lambda bi, i, *_: (0, 0))],
        out_specs=pl.BlockSpec((1, tm, d), lambda bi, i, *_: (bi, i, 0)),
        scratch_shapes=[pltpu.VMEM((2, chunk, half), jnp.uint32), pltpu.SemaphoreType.DMA((2,)),
                        pltpu.VMEM((tm, half), F32), pltpu.VMEM((tm, half), F32)],
    )
    return pl.pallas_call(
        functools.partial(_combine_kernel, n_total=n_total),
        grid_spec=grid_spec,
        out_shape=jax.ShapeDtypeStruct((b, l, d), F32),
        compiler_params=_params(2),
    )(start, n_chunks.astype(jnp.int32), first.astype(jnp.int32), sorted_rows, col(row_lo), col(row_hi),
      x, mod, final_g.reshape(1, d))


def kernel(x, c, ctx, c_ctx, w_mod, b_mod, norm_mix_g, w_in, b_gate, w_fourier, na_rel_bias,
           w_na_out, w_out, norm_ffn_g, w_router, w1, w3, w2, final_norm_g):
    assert w_mod.shape[0] == 1, "single-layer stack only"
    b, l, d = x.shape
    ctx_len = ctx.shape[1]
    f_width = w_fourier.shape[1]
    na_width = w_na_out.shape[1]
    n_experts = w_router.shape[-1]
    in_width = w_in.shape[-1]
    off_q = f_width
    off_k = off_q + na_width
    off_g = off_k + 2 * na_width
    cap = EC_FACTOR * l // n_experts
    SH_M, SC_M, GT_M, SH_F, SC_F, GT_F = range(N_MOD)

    pad = (-(b + 1)) % 16
    cond = jnp.concatenate([c, c_ctx[None], jnp.zeros((pad, d), F32)], axis=0)
    mod_all = _ada_params(cond, w_mod[0], b_mod[0])
    mod = mod_all[:b].reshape(b, 1, N_MOD * d)
    mod_ctx = mod_all[b:b + 1].reshape(1, 1, N_MOD * d)

    w_in_bf = w_in[0].astype(BF16)
    xn = _norm_mod(x, norm_mix_g[0], mod, SH_M, SC_M)
    cn = _norm_mod(ctx, norm_mix_g[0], mod_ctx, SH_M, SC_M)
    b_full = jnp.concatenate([jnp.zeros((off_g,), F32), b_gate[0]]).reshape(1, in_width)
    z = _proj_in(xn.reshape(b * l, d), w_in_bf, b_full, off_g)
    zc = _matmul_cols(cn.reshape(b * ctx_len, d), w_in_bf, off_k, 2 * na_width)
    z3 = z.reshape(b, l, in_width)

    ab = _fourier_chan(z3, f_width)
    y_four = _fourier_pos(ab.reshape(b, 2 * l, f_width))
    bias_slabs = _na_bias_slabs(na_rel_bias[0], l // GRID_W)
    o_na = _neighbourhood_attention(z3, zc.reshape(b, ctx_len, 2 * na_width), bias_slabs, off_q, na_width)
    mixed = _merge(y_four.reshape(b * l, f_width), o_na.reshape(b * l, na_width),
                   w_fourier[0], w_na_out[0], z, off_g)
    x1 = _out_proj_residual(mixed, w_out[0], x.reshape(b * l, d), mod, GT_M, l)

    x1_3 = x1.reshape(b, l, d)
    aff_t, xn_packed = _router(x1_3, norm_ffn_g[0], mod, SH_F, SC_F, w_router[0].T)
    idx_t, dst_t, gate_t, cum = _select(aff_t.reshape(b * n_experts, l), cap, n_experts)

    def expert_major(slot_major, batch_stride):
        v = slot_major.T.reshape(b, n_experts, cap)
        offs = (jnp.arange(b, dtype=jnp.int32) * batch_stride)[None, :, None]
        return (jnp.transpose(v, (1, 0, 2)) + offs.astype(v.dtype)).reshape(-1)

    per_batch = n_experts * cap
    xin = _gather_rows(expert_major(idx_t, l), xn_packed.reshape(b * l, d // 2))
    hid = _ffn_up(xin, w1[0], w3[0])
    out = _ffn_down(hid, w2[0], expert_major(gate_t, 0).reshape(-1, 1))
    sorted_rows = _permute_rows(out, expert_major(dst_t, per_batch))
    before = cum.reshape(b, n_experts, l)[:, 0, :] + (jnp.arange(b, dtype=F32) * per_batch)[:, None]
    after = jnp.concatenate([before[:, 1:], before[:, :1] + per_batch], axis=1)
    return _combine(sorted_rows, before, after, x1_3, mod, GT_F, final_norm_g)
```

```python
import functools

import numpy as np
import jax
import jax.numpy as jnp
from jax import lax
from jax.experimental import pallas as pl
from jax.experimental.pallas import tpu as pltpu

GRID_W = 64
F_GROUPS = 4
NA_HEAD_DIM = 128
NA_ROWS_MAX = 8
NA_COLS = 16
EC_FACTOR = 2
N_MOD = 6
RMS_EPS = 1e-6

SUBLANES = 8
COMBINE_CHUNK = 256
VMEM_LIMIT_BYTES = 56 * 1024 * 1024
BISECT_STEPS = 160
NA_ROWS_PER_STEP = 32
NA_HEADS_PER_STEP = 4
DMA_ISSUE_UNROLL = 32
SPLIT_BASE = 64
N_PIECES = 16

F32 = jnp.float32
BF16 = jnp.bfloat16
NT_DIMS = (((1,), (1,)), ((), ()))


def _params(n_axes):
    return pltpu.CompilerParams(
        dimension_semantics=("arbitrary",) * n_axes, vmem_limit_bytes=VMEM_LIMIT_BYTES)


def _tile(dim, pref):
    t = min(dim, pref)
    while dim % t:
        t -= 1
    return t


def _dot(a, b):
    return jnp.dot(a, b, preferred_element_type=F32)


def _rms_modulate(x, gain, scale, shift):
    ms = jnp.mean(x * x, axis=-1, keepdims=True)
    y = x * lax.rsqrt(ms + RMS_EPS) * gain
    return y * (1.0 + scale) + shift


def _ada_kernel(c_ref, w_ref, b_ref, o_ref):
    cond = c_ref[...]
    act = cond * jax.nn.sigmoid(cond)
    o_ref[...] = _dot(act.astype(BF16), w_ref[...].astype(BF16)) + b_ref[...]


def _ada_params(cond, w_mod, b_mod):
    m, d = cond.shape
    n = w_mod.shape[1]
    tn = _tile(n, 512)
    return pl.pallas_call(
        _ada_kernel,
        grid=(n // tn,),
        in_specs=[pl.BlockSpec((m, d), lambda j: (0, 0)),
                  pl.BlockSpec((d, tn), lambda j: (0, j)),
                  pl.BlockSpec((1, tn), lambda j: (0, j))],
        out_specs=pl.BlockSpec((m, tn), lambda j: (0, j)),
        out_shape=jax.ShapeDtypeStruct((m, n), F32),
        compiler_params=_params(1),
    )(cond, w_mod, b_mod.reshape(1, n))


def _norm_mod_kernel(x_ref, g_ref, sc_ref, sh_ref, o_ref):
    o_ref[0] = _rms_modulate(x_ref[0], g_ref[...], sc_ref[0], sh_ref[0]).astype(o_ref.dtype)


def _norm_mod(x, gain, mod, k_shift, k_scale):
    b, l, d = x.shape
    per_batch = mod.shape[0] == b
    tm = _tile(l, 512)
    bsel = (lambda i: i) if per_batch else (lambda i: 0)
    return pl.pallas_call(
        _norm_mod_kernel,
        grid=(b, l // tm),
        in_specs=[pl.BlockSpec((1, tm, d), lambda bi, i: (bi, i, 0)),
                  pl.BlockSpec((1, d), lambda bi, i: (0, 0)),
                  pl.BlockSpec((1, 1, d), lambda bi, i: (bsel(bi), 0, k_scale)),
                  pl.BlockSpec((1, 1, d), lambda bi, i: (bsel(bi), 0, k_shift))],
        out_specs=pl.BlockSpec((1, tm, d), lambda bi, i: (bi, i, 0)),
        out_shape=jax.ShapeDtypeStruct((b, l, d), BF16),
        compiler_params=_params(2),
    )(x, gain.reshape(1, d), mod, mod)


def _proj_in_kernel(a_ref, w_ref, b_ref, o_ref, *, gate_block0):
    acc = _dot(a_ref[...], w_ref[...])
    j = pl.program_id(1)

    @pl.when(j < gate_block0)
    def _():
        o_ref[...] = acc.astype(o_ref.dtype)

    @pl.when(j >= gate_block0)
    def _():
        o_ref[...] = jax.nn.sigmoid(acc + b_ref[...]).astype(o_ref.dtype)


def _proj_in(a, w, b_full, off_gate):
    m, k = a.shape
    n = w.shape[1]
    tm = _tile(m, 1024)
    tn = _tile(np.gcd(n, off_gate), 1024)
    return pl.pallas_call(
        functools.partial(_proj_in_kernel, gate_block0=off_gate // tn),
        grid=(m // tm, n // tn),
        in_specs=[pl.BlockSpec((tm, k), lambda i, j: (i, 0)),
                  pl.BlockSpec((k, tn), lambda i, j: (0, j)),
                  pl.BlockSpec((1, tn), lambda i, j: (0, j))],
        out_specs=pl.BlockSpec((tm, tn), lambda i, j: (i, j)),
        out_shape=jax.ShapeDtypeStruct((m, n), BF16),
        compiler_params=_params(2),
    )(a, w, b_full)


def _mm_kernel(a_ref, w_ref, o_ref):
    o_ref[...] = _dot(a_ref[...], w_ref[...]).astype(o_ref.dtype)


def _matmul_cols(a, w, col0, ncols):
    m, k = a.shape
    tm = _tile(m, 1024)
    tn = _tile(np.gcd(ncols, col0) if col0 else ncols, 512)
    j0 = col0 // tn
    return pl.pallas_call(
        _mm_kernel,
        grid=(m // tm, ncols // tn),
        in_specs=[pl.BlockSpec((tm, k), lambda i, j: (i, 0)),
                  pl.BlockSpec((k, tn), lambda i, j: (0, j0 + j))],
        out_specs=pl.BlockSpec((tm, tn), lambda i, j: (i, j)),
        out_shape=jax.ShapeDtypeStruct((m, ncols), BF16),
        compiler_params=_params(2),
    )(a, w)


def _dft_tables(n, scale):
    k = np.arange(n)
    ang = 2.0 * np.pi * ((k[:, None] * k[None, :]) % n) / n
    return np.cos(ang) * scale, np.sin(ang) * scale


def _fourier_chan_kernel(u_ref, t_ref, o_ref):
    dg = u_ref.shape[-1]
    res = _dot(u_ref[0], t_ref[...])
    o_ref[0, 0] = res[:, :dg].astype(o_ref.dtype)
    o_ref[0, 1] = res[:, dg:].astype(o_ref.dtype)


def _fourier_chan(z3, f_width):
    b, l, _ = z3.shape
    dg = f_width // F_GROUPS
    cd, sd = _dft_tables(dg, dg ** -0.5)
    table = jnp.asarray(np.concatenate([cd, sd], axis=1), dtype=BF16)
    tm = _tile(l, 1024)
    return pl.pallas_call(
        _fourier_chan_kernel,
        grid=(b, l // tm, F_GROUPS),
        in_specs=[pl.BlockSpec((1, tm, dg), lambda bi, i, g: (bi, i, g)),
                  pl.BlockSpec((dg, 2 * dg), lambda bi, i, g: (0, 0))],
        out_specs=pl.BlockSpec((1, 2, tm, dg), lambda bi, i, g: (bi, 0, i, g)),
        out_shape=jax.ShapeDtypeStruct((b, 2, l, f_width), BF16),
        compiler_params=_params(3),
    )(z3, table)


def _fourier_pos_kernel(t_ref, ab_ref, o_ref):
    o_ref[0] = _dot(t_ref[...], ab_ref[0]).astype(o_ref.dtype)


def _fourier_pos(ab):
    b, l2, fw = ab.shape
    l = l2 // 2
    cl, sl = _dft_tables(l, l ** -0.5)
    table = jnp.asarray(np.concatenate([cl, -sl], axis=1), dtype=BF16)
    tm = _tile(l, 1024)
    tn = _tile(fw, 512)
    return pl.pallas_call(
        _fourier_pos_kernel,
        grid=(b, fw // tn, l // tm),
        in_specs=[pl.BlockSpec((tm, l2), lambda bi, j, i: (i, 0)),
                  pl.BlockSpec((1, l2, tn), lambda bi, j, i: (bi, 0, j))],
        out_specs=pl.BlockSpec((1, tm, tn), lambda bi, j, i: (bi, i, j)),
        out_shape=jax.ShapeDtypeStruct((b, l, fw), BF16),
        compiler_params=_params(3),
    )(table, ab)


def _na_geometry(rows):
    kr = min(NA_ROWS_MAX, rows)
    starts = [int(np.clip(r - kr // 2, 0, rows - kr)) - r + NA_ROWS_MAX - 1 for r in range(rows)]
    return kr, min(starts), max(starts) - min(starts) + 1


def _na_bias_slabs(rel_bias, rows):
    kr, min_start, n_slabs = _na_geometry(rows)
    cols = np.arange(GRID_W)
    win_start = np.clip(cols - NA_COLS // 2, 0, GRID_W - NA_COLS)
    rel_col = cols[None, :] - win_start[:, None]
    col_mask = (rel_col >= 0) & (rel_col < NA_COLS)
    dc_idx = np.clip(cols[None, :] - cols[:, None] + NA_COLS - 1, 0, 2 * NA_COLS - 2)
    neg = jnp.finfo(F32).min
    bias_c = jnp.where(col_mask[None, None], rel_bias.astype(F32)[:, :, dc_idx], neg)
    slabs = jnp.stack([bias_c[:, min_start + s:min_start + s + kr] for s in range(n_slabs)], axis=1)
    slabs = jnp.transpose(slabs, (0, 1, 3, 2, 4))
    h = rel_bias.shape[0]
    return slabs.reshape(h, n_slabs, GRID_W, kr * GRID_W)


def _na_kernel(q_ref, k_ref, v_ref, kc_ref, vc_ref, bias_ref, o_ref, s_ctx_all, p_ctx_all, o_loc_all,
               *, heads_per_step, **geometry):
    for hh in range(heads_per_step):
        cols = slice(hh * NA_HEAD_DIM, (hh + 1) * NA_HEAD_DIM)
        _na_one_head(q_ref.at[0, :, cols], k_ref.at[0, :, cols], v_ref.at[0, :, cols], kc_ref.at[0, :, cols],
                     vc_ref.at[0, :, cols], bias_ref.at[hh], o_ref.at[0, :, cols],
                     s_ctx_all, p_ctx_all, o_loc_all, **geometry)


def _na_one_head(q_ref, k_ref, v_ref, kc_ref, vc_ref, bias_ref, o_ref, s_ctx_all, p_ctx_all, o_loc_all,
                 *, rows, rows_per_step, kr, min_start, scale):
    s_ctx_all[...] = lax.dot_general(q_ref[...], kc_ref[...], NT_DIMS, preferred_element_type=F32) * scale

    def scores(r):
        rs = jnp.clip(r - kr // 2, 0, rows - kr)
        slab = rs - r + (NA_ROWS_MAX - 1 - min_start)
        q0 = pl.multiple_of(r * GRID_W, GRID_W)
        k0 = pl.multiple_of(rs * GRID_W, GRID_W)
        qr = q_ref[pl.ds(q0, GRID_W), :]
        kw = k_ref[pl.ds(k0, kr * GRID_W), :]
        s_loc = lax.dot_general(qr, kw, NT_DIMS, preferred_element_type=F32) * scale + bias_ref[slab]
        return q0, k0, s_loc

    def softmax(q0, s_loc):
        s_ctx = s_ctx_all[pl.ds(q0, GRID_W), :]
        m = jnp.maximum(jnp.max(s_loc, axis=-1, keepdims=True), jnp.max(s_ctx, axis=-1, keepdims=True))
        p_loc = jnp.exp(s_loc - m)
        p_ctx = jnp.exp(s_ctx - m)
        inv = 1.0 / (jnp.sum(p_loc, axis=-1, keepdims=True) + jnp.sum(p_ctx, axis=-1, keepdims=True))
        return (p_loc * inv).astype(BF16), (p_ctx * inv).astype(BF16)

    def row_group(g, carry):
        sc = [scores(g * rows_per_step + u) for u in range(rows_per_step)]
        pr = [softmax(q0, s_loc) for q0, _, s_loc in sc]
        for (q0, k0, _), (p_loc, p_ctx) in zip(sc, pr):
            p_ctx_all[pl.ds(q0, GRID_W), :] = p_ctx
            o_loc_all[pl.ds(q0, GRID_W), :] = _dot(p_loc, v_ref[pl.ds(k0, kr * GRID_W), :])
        return carry

    lax.fori_loop(0, rows // rows_per_step, row_group, 0)
    o_ref[...] = (o_loc_all[...] + _dot(p_ctx_all[...], vc_ref[...])).astype(o_ref.dtype)


def _neighbourhood_attention(z3, zc3, bias_slabs, off_q, na_width):
    b, l, _ = z3.shape
    ctx_len = zc3.shape[1]
    dh = NA_HEAD_DIM
    heads = na_width // dh
    rows = l // GRID_W
    kr, min_start, n_slabs = _na_geometry(rows)
    hp = _tile(heads, NA_HEADS_PER_STEP)
    w = hp * dh
    qb, kb, vb = off_q // w, (off_q + na_width) // w, (off_q + 2 * na_width) // w
    kern = functools.partial(_na_kernel, heads_per_step=hp, rows=rows,
                             rows_per_step=_tile(rows, NA_ROWS_PER_STEP), kr=kr,
                             min_start=min_start, scale=dh ** -0.5)
    return pl.pallas_call(
        kern,
        grid=(heads // hp, b),
        in_specs=[pl.BlockSpec((1, l, w), lambda h, bi: (bi, 0, qb + h)),
                  pl.BlockSpec((1, l, w), lambda h, bi: (bi, 0, kb + h)),
                  pl.BlockSpec((1, l, w), lambda h, bi: (bi, 0, vb + h)),
                  pl.BlockSpec((1, ctx_len, w), lambda h, bi: (bi, 0, h)),
                  pl.BlockSpec((1, ctx_len, w), lambda h, bi: (bi, 0, heads // hp + h)),
                  pl.BlockSpec((hp, n_slabs, GRID_W, kr * GRID_W), lambda h, bi: (h, 0, 0, 0))],
        out_specs=pl.BlockSpec((1, l, w), lambda h, bi: (bi, 0, h)),
        out_shape=jax.ShapeDtypeStruct((b, l, na_width), BF16),
        scratch_shapes=[pltpu.VMEM((l, ctx_len), F32), pltpu.VMEM((l, ctx_len), BF16),
                        pltpu.VMEM((l, dh), F32)],
        compiler_params=_params(2),
    )(z3, z3, z3, zc3, zc3, bias_slabs)


def _merge_kernel(yf_ref, on_ref, wf_ref, wn_ref, gf_ref, gn_ref, o_ref, wf_s, wn_s):
    @pl.when(pl.program_id(1) == 0)
    def _():
        wf_s[...] = wf_ref[...].astype(BF16)
        wn_s[...] = wn_ref[...].astype(BF16)

    y_f = _dot(yf_ref[...], wf_s[...])
    y_n = _dot(on_ref[...], wn_s[...])
    o_ref[...] = (gf_ref[...].astype(F32) * y_f + gn_ref[...].astype(F32) * y_n).astype(o_ref.dtype)


def _merge(y_four, o_na, w_f, w_n, z, off_gate):
    m, kf = y_four.shape
    kn = o_na.shape[1]
    d = w_f.shape[1]
    tm = _tile(m, 1024)
    tn = _tile(np.gcd(d, off_gate), 512)
    gf0, gn0 = off_gate // tn, (off_gate + d) // tn
    return pl.pallas_call(
        _merge_kernel,
        grid=(d // tn, m // tm),
        in_specs=[pl.BlockSpec((tm, kf), lambda j, i: (i, 0)),
                  pl.BlockSpec((tm, kn), lambda j, i: (i, 0)),
                  pl.BlockSpec((kf, tn), lambda j, i: (0, j)),
                  pl.BlockSpec((kn, tn), lambda j, i: (0, j)),
                  pl.BlockSpec((tm, tn), lambda j, i: (i, gf0 + j)),
                  pl.BlockSpec((tm, tn), lambda j, i: (i, gn0 + j))],
        out_specs=pl.BlockSpec((tm, tn), lambda j, i: (i, j)),
        out_shape=jax.ShapeDtypeStruct((m, d), BF16),
        scratch_shapes=[pltpu.VMEM((kf, tn), BF16), pltpu.VMEM((kn, tn), BF16)],
        compiler_params=_params(2),
    )(y_four, o_na, w_f, w_n, z, z)


def _out_proj_kernel(a_ref, w_ref, x_ref, gt_ref, o_ref, w_s):
    @pl.when(pl.program_id(1) == 0)
    def _():
        w_s[...] = w_ref[...].astype(BF16)

    o_ref[...] = x_ref[...] + gt_ref[0] * _dot(a_ref[...], w_s[...])


def _out_proj_residual(a, w, x2d, mod, k_gate, seq):
    m, k = a.shape
    d = w.shape[1]
    tm = _tile(seq, 1024)
    tn = _tile(d, 512)
    gate0 = k_gate * (d // tn)
    return pl.pallas_call(
        _out_proj_kernel,
        grid=(d // tn, m // tm),
        in_specs=[pl.BlockSpec((tm, k), lambda j, i: (i, 0)),
                  pl.BlockSpec((k, tn), lambda j, i: (0, j)),
                  pl.BlockSpec((tm, tn), lambda j, i: (i, j)),
                  pl.BlockSpec((1, 1, tn), lambda j, i: (i * tm // seq, 0, gate0 + j))],
        out_specs=pl.BlockSpec((tm, tn), lambda j, i: (i, j)),
        out_shape=jax.ShapeDtypeStruct((m, d), F32),
        scratch_shapes=[pltpu.VMEM((k, tn), BF16)],
        compiler_params=_params(2),
    )(a, w, x2d, mod)


def _router_kernel(x_ref, g_ref, sc_ref, sh_ref, wr_ref, aff_ref, xp_ref):
    xn = _rms_modulate(x_ref[0], g_ref[...], sc_ref[0], sh_ref[0])
    half = xn.shape[1] // 2
    xp_ref[0] = _pack_bf16_pair(xn[:, :half], xn[:, half:])
    x_hi = xn.astype(BF16)
    x_lo = (xn - x_hi.astype(F32)).astype(BF16)
    w = wr_ref[...]
    w_hi = w.astype(BF16)
    w_lo = (w - w_hi.astype(F32)).astype(BF16)
    logits = (lax.dot_general(w_hi, x_hi, NT_DIMS, preferred_element_type=F32)
              + lax.dot_general(w_lo, x_hi, NT_DIMS, preferred_element_type=F32)
              + lax.dot_general(w_hi, x_lo, NT_DIMS, preferred_element_type=F32))
    ex = jnp.exp(logits - jnp.max(logits, axis=0, keepdims=True))
    aff_ref[0] = ex / jnp.sum(ex, axis=0, keepdims=True)


def _router(x, gain, mod, k_shift, k_scale, w_router_t):
    b, l, d = x.shape
    e = w_router_t.shape[0]
    tm = _tile(l, 256)
    return pl.pallas_call(
        _router_kernel,
        grid=(b, l // tm),
        in_specs=[pl.BlockSpec((1, tm, d), lambda bi, i: (bi, i, 0)),
                  pl.BlockSpec((1, d), lambda bi, i: (0, 0)),
                  pl.BlockSpec((1, 1, d), lambda bi, i: (bi, 0, k_scale)),
                  pl.BlockSpec((1, 1, d), lambda bi, i: (bi, 0, k_shift)),
                  pl.BlockSpec((e, d), lambda bi, i: (0, 0))],
        out_specs=[pl.BlockSpec((1, e, tm), lambda bi, i: (bi, 0, i)),
                   pl.BlockSpec((1, tm, d // 2), lambda bi, i: (bi, i, 0))],
        out_shape=[jax.ShapeDtypeStruct((b, e, l), F32), jax.ShapeDtypeStruct((b, l, d // 2), jnp.uint32)],
        compiler_params=_params(2),
    )(x, gain.reshape(1, d), mod, mod, w_router_t)


def _select_kernel(a_ref, tri_ref, same_ref, lower_ref, idx_ref, dst_ref, gate_ref, cum_ref, pos_ref, rank_ref,
                   *, cap):
    a = a_ref[...]
    n_rows, l = a.shape
    capf = float(cap)

    def count(mask):
        return jnp.sum(jnp.where(mask, 1.0, 0.0), axis=-1, keepdims=True)

    def halve(_, lo_hi):
        lo, hi = lo_hi
        mid = 0.5 * lo + 0.5 * hi
        ok = count(a_ref[...] >= mid) >= capf
        return jnp.where(ok, mid, lo), jnp.where(ok, hi, mid)

    _, hi = lax.fori_loop(0, BISECT_STEPS, halve,
                          (jnp.zeros((n_rows, 1), F32), jnp.full((n_rows, 1), 2.0, F32)))
    tau = jnp.max(jnp.where(a < hi, a, -1.0), axis=-1, keepdims=True)
    above = a > tau
    tied = a == tau
    need = capf - count(above)
    tri = tri_ref[...]
    tied_before = _dot(jnp.where(tied, 1.0, 0.0).astype(BF16), tri)
    chosen = above | (tied & (tied_before < need))
    chosen_b = jnp.where(chosen, 1.0, 0.0).astype(BF16)
    pos = _dot(chosen_b, tri)
    pos_ref[...] = jnp.where(chosen, pos, -1.0)
    per_token = _dot(same_ref[...], chosen_b)
    before_token = _dot(per_token.astype(BF16), tri)
    cum_ref[...] = before_token
    rank_ref[...] = before_token + _dot(lower_ref[...], chosen_b)

    split = float(SPLIT_BASE)
    slot = lax.broadcasted_iota(jnp.int32, (cap, l), 0).astype(F32)
    piece = lax.broadcasted_iota(jnp.int32, (N_PIECES, l), 0)
    tok = lax.broadcasted_iota(jnp.int32, (1, l), 1).astype(F32)
    tok_hi = jnp.floor(tok * (1.0 / split))
    tok_lo = tok - split * tok_hi

    def compact(r, carry):
        hit = jnp.where(pos_ref[pl.ds(r, 1), :] == slot, 1.0, 0.0).astype(BF16)
        rank = rank_ref[pl.ds(r, 1), :]
        rank_hi = jnp.floor(rank * (1.0 / split))
        aff = a_ref[pl.ds(r, 1), :]
        aff_1 = aff.astype(BF16).astype(F32)
        aff_2 = (aff - aff_1).astype(BF16).astype(F32)
        rows = (tok_hi, tok_lo, rank_hi, rank - split * rank_hi, aff_1, aff_2, aff - aff_1 - aff_2)
        vals = jnp.zeros((N_PIECES, l), F32)
        for k, v in enumerate(rows):
            vals = jnp.where(piece == k, v, vals)
        got = lax.dot_general(vals.astype(BF16), hit, NT_DIMS, preferred_element_type=F32)
        idx_ref[pl.ds(r, 1), :] = (split * got[0:1] + got[1:2]).astype(jnp.int32)
        dst_ref[pl.ds(r, 1), :] = (split * got[2:3] + got[3:4]).astype(jnp.int32)
        gate_ref[pl.ds(r, 1), :] = got[4:5] + got[5:6] + got[6:7]
        return carry

    lax.fori_loop(0, n_rows, compact, 0)


def _select(aff_rows, cap, n_experts):
    n_rows, l = aff_rows.shape
    tri = jnp.asarray(np.triu(np.ones((l, l), np.float32), k=1), dtype=BF16)
    group = np.arange(n_rows) // n_experts
    same = group[:, None] == group[None, :]
    lower = same & (np.arange(n_rows)[None, :] < np.arange(n_rows)[:, None])
    assert l <= SPLIT_BASE ** 2 and n_experts * cap <= SPLIT_BASE ** 2
    full = lambda shape: pl.BlockSpec(shape, lambda i: (0, 0))
    per_slot = jax.ShapeDtypeStruct((n_rows, cap), jnp.int32)
    return pl.pallas_call(
        functools.partial(_select_kernel, cap=cap),
        grid=(1,),
        in_specs=[full((n_rows, l)), full((l, l)), full((n_rows, n_rows)), full((n_rows, n_rows))],
        out_specs=[full((n_rows, cap)), full((n_rows, cap)), full((n_rows, cap)), full((n_rows, l))],
        out_shape=[per_slot, per_slot, jax.ShapeDtypeStruct((n_rows, cap), F32),
                   jax.ShapeDtypeStruct((n_rows, l), F32)],
        scratch_shapes=[pltpu.VMEM((n_rows, l), F32), pltpu.VMEM((n_rows, l), F32)],
        compiler_params=_params(1),
    )(aff_rows, tri, jnp.asarray(same, dtype=BF16), jnp.asarray(lower, dtype=BF16))


def _gather_rows_kernel(rows_ref, src_hbm, o_ref, sem):
    n_rows = o_ref.shape[0]
    base = pl.program_id(0) * n_rows

    def start(j, carry):
        pltpu.make_async_copy(src_hbm.at[pl.ds(rows_ref[base + j], 1)], o_ref.at[pl.ds(j, 1)], sem).start()
        return carry

    lax.fori_loop(0, n_rows, start, 0, unroll=DMA_ISSUE_UNROLL)
    pltpu.make_async_copy(src_hbm.at[pl.ds(0, n_rows)], o_ref, sem).wait()


def _gather_rows(flat_rows, src):
    n = flat_rows.shape[0]
    w = src.shape[1]
    r = _tile(n, 1024)
    grid_spec = pltpu.PrefetchScalarGridSpec(
        num_scalar_prefetch=1,
        grid=(n // r,),
        in_specs=[pl.BlockSpec(memory_space=pl.ANY)],
        out_specs=pl.BlockSpec((r, w), lambda i, rows: (i, 0)),
        scratch_shapes=[pltpu.SemaphoreType.DMA],
    )
    return pl.pallas_call(
        _gather_rows_kernel,
        grid_spec=grid_spec,
        out_shape=jax.ShapeDtypeStruct((n, w), src.dtype),
        compiler_params=_params(1),
    )(flat_rows, src)


def _ffn_up_kernel(xp_ref, w1_ref, w3_ref, o_ref, x_s):
    half = xp_ref.shape[1]

    @pl.when(pl.program_id(2) == 0)
    def _():
        lo, hi = _unpack_bf16_pair(xp_ref[...])
        x_s[:, :half] = lo
        x_s[:, half:] = hi

    xin = x_s[...]
    a = _dot(xin, w1_ref[0].astype(BF16))
    o_ref[...] = (a * jax.nn.sigmoid(a) * _dot(xin, w3_ref[0].astype(BF16))).astype(o_ref.dtype)


def _ffn_up(xin_packed, w1, w3):
    n, half = xin_packed.shape
    e, d, ff = w1.shape
    rows_e = n // e
    tm = _tile(rows_e, 1024)
    tn = _tile(ff, 256)
    per_e = rows_e // tm
    return pl.pallas_call(
        _ffn_up_kernel,
        grid=(e, per_e, ff // tn),
        in_specs=[pl.BlockSpec((tm, half), lambda ei, i, j: (ei * per_e + i, 0)),
                  pl.BlockSpec((1, d, tn), lambda ei, i, j: (ei, 0, j)),
                  pl.BlockSpec((1, d, tn), lambda ei, i, j: (ei, 0, j))],
        out_specs=pl.BlockSpec((tm, tn), lambda ei, i, j: (ei * per_e + i, j)),
        out_shape=jax.ShapeDtypeStruct((n, ff), BF16),
        scratch_shapes=[pltpu.VMEM((tm, d), BF16)],
        compiler_params=_params(3),
    )(xin_packed, w1, w3)


def _bf16_bits(x):
    return lax.bitcast_convert_type(x.astype(BF16).astype(F32), jnp.uint32)


def _pack_bf16_pair(lo, hi):
    return (_bf16_bits(lo) >> 16) | _bf16_bits(hi)


def _unpack_bf16_pair(words):
    lo = lax.bitcast_convert_type(words << 16, F32).astype(BF16)
    hi = lax.bitcast_convert_type(words & jnp.uint32(0xFFFF0000), F32).astype(BF16)
    return lo, hi


def _ffn_down_kernel(h_ref, wa_ref, wb_ref, g_ref, o_ref):
    h = h_ref[...]
    gate = g_ref[...]
    o_ref[...] = _pack_bf16_pair(_dot(h, wa_ref[0].astype(BF16)) * gate,
                                 _dot(h, wb_ref[0].astype(BF16)) * gate)


def _ffn_down(hid, w2, gate_col):
    n, ff = hid.shape
    e, _, d = w2.shape
    rows_e = n // e
    half = d // 2
    tw = _tile(half, 256)
    nb = half // tw
    return pl.pallas_call(
        _ffn_down_kernel,
        grid=(e, nb),
        in_specs=[pl.BlockSpec((rows_e, ff), lambda ei, j: (ei, 0)),
                  pl.BlockSpec((1, ff, tw), lambda ei, j: (ei, 0, j)),
                  pl.BlockSpec((1, ff, tw), lambda ei, j: (ei, 0, nb + j)),
                  pl.BlockSpec((rows_e, 1), lambda ei, j: (ei, 0))],
        out_specs=pl.BlockSpec((rows_e, tw), lambda ei, j: (ei, j)),
        out_shape=jax.ShapeDtypeStruct((n, half), jnp.uint32),
        compiler_params=_params(2),
    )(hid, w2, w2, gate_col)


def _row_copy(src, dst_hbm, sem, j, row):
    return pltpu.make_async_copy(src.at[pl.ds(j, 1)], dst_hbm.at[pl.ds(row, 1)], sem)


def _permute_kernel(dest_ref, src_ref, dst_hbm, sem):
    n_rows = src_ref.shape[0]
    base = pl.program_id(0) * n_rows

    def start(j, carry):
        _row_copy(src_ref, dst_hbm, sem, j, dest_ref[base + j]).start()
        return carry

    lax.fori_loop(0, n_rows, start, 0, unroll=DMA_ISSUE_UNROLL)
    pltpu.make_async_copy(src_ref, dst_hbm.at[pl.ds(0, n_rows)], sem).wait()


def _permute_rows(rows, dest):
    n, w = rows.shape
    r = _tile(n, 1024)
    grid_spec = pltpu.PrefetchScalarGridSpec(
        num_scalar_prefetch=1,
        grid=(n // r,),
        in_specs=[pl.BlockSpec((r, w), lambda i, dest_ref: (i, 0))],
        out_specs=pl.BlockSpec(memory_space=pl.ANY),
        scratch_shapes=[pltpu.SemaphoreType.DMA],
    )
    return pl.pallas_call(
        _permute_kernel,
        grid_spec=grid_spec,
        out_shape=jax.ShapeDtypeStruct((n, w), rows.dtype),
        compiler_params=_params(1),
    )(dest, rows)


def _combine_kernel(start_ref, nchunks_ref, first_ref, rows_hbm, lo_ref, hi_ref, x_ref, gt_ref, fg_ref,
                    o_ref, buf, sem, acc_lo, acc_hi, *, n_total):
    step = pl.program_id(0) * pl.num_programs(1) + pl.program_id(1)
    n_steps = pl.num_programs(0) * pl.num_programs(1)
    chunk = buf.shape[1]
    tm = acc_lo.shape[0]
    n_chunks = nchunks_ref[step]
    first = first_ref[step]

    def row0(s, c):
        return pl.multiple_of(jnp.minimum(start_ref[s] + c * chunk, n_total - chunk), SUBLANES)

    def fetch(s, c, slot):
        return pltpu.make_async_copy(rows_hbm.at[pl.ds(row0(s, c), chunk)], buf.at[slot], sem.at[slot])

    @pl.when(step == 0)
    def _():
        fetch(0, 0, 0).start()

    acc_lo[...] = jnp.zeros_like(acc_lo)
    acc_hi[...] = jnp.zeros_like(acc_hi)
    row_lo = lo_ref[0]
    row_hi = hi_ref[0]

    def consume(c, carry):
        slot = (first + c) % 2
        ends_step = c + 1 == n_chunks

        @pl.when(jnp.logical_not(ends_step))
        def _():
            fetch(step, c + 1, 1 - slot).start()

        @pl.when(jnp.logical_and(ends_step, step + 1 < n_steps))
        def _():
            fetch(step + 1, 0, 1 - slot).start()

        fetch(step, c, slot).wait()
        lo, hi = _unpack_bf16_pair(buf[slot])
        row = (row0(step, c) + lax.broadcasted_iota(jnp.int32, (tm, chunk), 1)).astype(F32)
        first_row = jnp.maximum(row_lo, (start_ref[step] + c * chunk).astype(F32))
        onehot = jnp.where(jnp.logical_and(row >= first_row, row < row_hi), 1.0, 0.0).astype(BF16)
        acc_lo[...] += _dot(onehot, lo)
        acc_hi[...] += _dot(onehot, hi)
        return carry

    lax.fori_loop(0, n_chunks, consume, 0)
    y = jnp.concatenate([acc_lo[...], acc_hi[...]], axis=-1)
    x2 = x_ref[0] + gt_ref[0] * y
    ms = jnp.mean(x2 * x2, axis=-1, keepdims=True)
    o_ref[0] = x2 * lax.rsqrt(ms + RMS_EPS) * fg_ref[...]


def _combine(sorted_rows, row_lo, row_hi, x, mod, k_gate, final_g):
    b, l, d = x.shape
    n_total, half = sorted_rows.shape
    tm = _tile(l, 256)
    nt = l // tm
    chunk = COMBINE_CHUNK
    assert n_total >= chunk and n_total % SUBLANES == 0
    tile_lo = row_lo[:, ::tm].astype(jnp.int32).reshape(-1)
    tile_hi = row_hi[:, tm - 1::tm].astype(jnp.int32).reshape(-1)
    start = tile_lo // SUBLANES * SUBLANES
    n_chunks = jnp.maximum(1, (tile_hi - start + chunk - 1) // chunk)
    first = jnp.cumsum(n_chunks) - n_chunks
    col = lambda a: a.reshape(b, l, 1)
    grid_spec = pltpu.PrefetchScalarGridSpec(
        num_scalar_prefetch=3,
        grid=(b, nt),
        in_specs=[pl.BlockSpec(memory_space=pl.ANY),
                  pl.BlockSpec((1, tm, 1), lambda bi, i, *_: (bi, i, 0)),
                  pl.BlockSpec((1, tm, 1), lambda bi, i, *_: (bi, i, 0)),
                  pl.BlockSpec((1, tm, d), lambda bi, i, *_: (bi, i, 0)),
                  pl.BlockSpec((1, 1, d), lambda bi, i, *_: (bi, 0, k_gate)),
                  pl.BlockSpec((1, d), lambda bi, i, *_: (0, 0))],
        out_specs=pl.BlockSpec((1, tm, d), lambda bi, i, *_: (bi, i, 0)),
        scratch_shapes=[pltpu.VMEM((2, chunk, half), jnp.uint32), pltpu.SemaphoreType.DMA((2,)),
                        pltpu.VMEM((tm, half), F32), pltpu.VMEM((tm, half), F32)],
    )
    return pl.pallas_call(
        functools.partial(_combine_kernel, n_total=n_total),
        grid_spec=grid_spec,
        out_shape=jax.ShapeDtypeStruct((b, l, d), F32),
        compiler_params=_params(2),
    )(start, n_chunks.astype(jnp.int32), first.astype(jnp.int32), sorted_rows, col(row_lo), col(row_hi),
      x, mod, final_g.reshape(1, d))


def kernel(x, c, ctx, c_ctx, w_mod, b_mod, norm_mix_g, w_in, b_gate, w_fourier, na_rel_bias,
           w_na_out, w_out, norm_ffn_g, w_router, w1, w3, w2, final_norm_g):
    assert w_mod.shape[0] == 1, "single-layer stack only"
    b, l, d = x.shape
    ctx_len = ctx.shape[1]
    f_width = w_fourier.shape[1]
    na_width = w_na_out.shape[1]
    n_experts = w_router.shape[-1]
    in_width = w_in.shape[-1]
    off_q = f_width
    off_k = off_q + na_width
    off_g = off_k + 2 * na_width
    cap = EC_FACTOR * l // n_experts
    SH_M, SC_M, GT_M, SH_F, SC_F, GT_F = range(N_MOD)

    pad = (-(b + 1)) % 16
    cond = jnp.concatenate([c, c_ctx[None], jnp.zeros((pad, d), F32)], axis=0)
    mod_all = _ada_params(cond, w_mod[0], b_mod[0])
    mod = mod_all[:b].reshape(b, 1, N_MOD * d)
    mod_ctx = mod_all[b:b + 1].reshape(1, 1, N_MOD * d)

    w_in_bf = w_in[0].astype(BF16)
    xn = _norm_mod(x, norm_mix_g[0], mod, SH_M, SC_M)
    cn = _norm_mod(ctx, norm_mix_g[0], mod_ctx, SH_M, SC_M)
    b_full = jnp.concatenate([jnp.zeros((off_g,), F32), b_gate[0]]).reshape(1, in_width)
    z = _proj_in(xn.reshape(b * l, d), w_in_bf, b_full, off_g)
    zc = _matmul_cols(cn.reshape(b * ctx_len, d), w_in_bf, off_k, 2 * na_width)
    z3 = z.reshape(b, l, in_width)

    ab = _fourier_chan(z3, f_width)
    y_four = _fourier_pos(ab.reshape(b, 2 * l, f_width))
    bias_slabs = _na_bias_slabs(na_rel_bias[0], l // GRID_W)
    o_na = _neighbourhood_attention(z3, zc.reshape(b, ctx_len, 2 * na_width), bias_slabs, off_q, na_width)
    mixed = _merge(y_four.reshape(b * l, f_width), o_na.reshape(b * l, na_width),
                   w_fourier[0], w_na_out[0], z, off_g)
    x1 = _out_proj_residual(mixed, w_out[0], x.reshape(b * l, d), mod, GT_M, l)

    x1_3 = x1.reshape(b, l, d)
    aff_t, xn_packed = _router(x1_3, norm_ffn_g[0], mod, SH_F, SC_F, w_router[0].T)
    idx_t, dst_t, gate_t, cum = _select(aff_t.reshape(b * n_experts, l), cap, n_experts)

    def expert_major(per_slot, batch_stride):
        v = per_slot.reshape(b, n_experts, cap)
        offs = (jnp.arange(b, dtype=jnp.int32) * batch_stride)[None, :, None]
        return (jnp.transpose(v, (1, 0, 2)) + offs.astype(v.dtype)).reshape(-1)

    per_batch = n_experts * cap
    xin = _gather_rows(expert_major(idx_t, l), xn_packed.reshape(b * l, d // 2))
    hid = _ffn_up(xin, w1[0], w3[0])
    out = _ffn_down(hid, w2[0], expert_major(gate_t, 0).reshape(-1, 1))
    sorted_rows = _permute_rows(out, expert_major(dst_t, per_batch))
    before = cum.reshape(b, n_experts, l)[:, 0, :] + (jnp.arange(b, dtype=F32) * per_batch)[:, None]
    after = jnp.concatenate([before[:, 1:], before[:, :1] + per_batch], axis=1)
    return _combine(sorted_rows, before, after, x1_3, mod, GT_F, final_norm_g)
```

```python
import functools

import numpy as np
import jax
import jax.numpy as jnp
from jax import lax
from jax.experimental import pallas as pl
from jax.experimental.pallas import tpu as pltpu

GRID_W = 64
F_GROUPS = 4
NA_HEAD_DIM = 128
NA_ROWS_MAX = 8
NA_COLS = 16
EC_FACTOR = 2
N_MOD = 6
RMS_EPS = 1e-6

SUBLANES = 8
COMBINE_CHUNK = 256
VMEM_LIMIT_BYTES = 56 * 1024 * 1024
BISECT_STEPS = 160
NA_ROWS_PER_STEP = 32
NA_HEADS_PER_STEP = 4
DMA_ISSUE_UNROLL = 32
SPLIT_BASE = 64
N_PIECES = 16

F32 = jnp.float32
BF16 = jnp.bfloat16
NT_DIMS = (((1,), (1,)), ((), ()))


def _params(n_axes):
    return pltpu.CompilerParams(
        dimension_semantics=("arbitrary",) * n_axes, vmem_limit_bytes=VMEM_LIMIT_BYTES)


def _tile(dim, pref):
    t = min(dim, pref)
    while dim % t:
        t -= 1
    return t


def _dot(a, b):
    return jnp.dot(a, b, preferred_element_type=F32)


def _rms_modulate(x, gain, scale, shift):
    ms = jnp.mean(x * x, axis=-1, keepdims=True)
    y = x * lax.rsqrt(ms + RMS_EPS) * gain
    return y * (1.0 + scale) + shift


def _ada_kernel(c_ref, w_ref, b_ref, o_ref):
    cond = c_ref[...]
    act = cond * jax.nn.sigmoid(cond)
    o_ref[...] = _dot(act.astype(BF16), w_ref[...].astype(BF16)) + b_ref[...]


def _ada_params(cond, w_mod, b_mod):
    m, d = cond.shape
    n = w_mod.shape[1]
    tn = _tile(n, 512)
    return pl.pallas_call(
        _ada_kernel,
        grid=(n // tn,),
        in_specs=[pl.BlockSpec((m, d), lambda j: (0, 0)),
                  pl.BlockSpec((d, tn), lambda j: (0, j)),
                  pl.BlockSpec((1, tn), lambda j: (0, j))],
        out_specs=pl.BlockSpec((m, tn), lambda j: (0, j)),
        out_shape=jax.ShapeDtypeStruct((m, n), F32),
        compiler_params=_params(1),
    )(cond, w_mod, b_mod.reshape(1, n))


def _norm_mod_kernel(x_ref, g_ref, sc_ref, sh_ref, o_ref):
    o_ref[0] = _rms_modulate(x_ref[0], g_ref[...], sc_ref[0], sh_ref[0]).astype(o_ref.dtype)


def _norm_mod(x, gain, mod, k_shift, k_scale):
    b, l, d = x.shape
    per_batch = mod.shape[0] == b
    tm = _tile(l, 512)
    bsel = (lambda i: i) if per_batch else (lambda i: 0)
    return pl.pallas_call(
        _norm_mod_kernel,
        grid=(b, l // tm),
        in_specs=[pl.BlockSpec((1, tm, d), lambda bi, i: (bi, i, 0)),
                  pl.BlockSpec((1, d), lambda bi, i: (0, 0)),
                  pl.BlockSpec((1, 1, d), lambda bi, i: (bsel(bi), 0, k_scale)),
                  pl.BlockSpec((1, 1, d), lambda bi, i: (bsel(bi), 0, k_shift))],
        out_specs=pl.BlockSpec((1, tm, d), lambda bi, i: (bi, i, 0)),
        out_shape=jax.ShapeDtypeStruct((b, l, d), BF16),
        compiler_params=_params(2),
    )(x, gain.reshape(1, d), mod, mod)


def _proj_in_kernel(a_ref, w_ref, b_ref, o_ref, *, gate_block0):
    acc = _dot(a_ref[...], w_ref[...])
    j = pl.program_id(1)

    @pl.when(j < gate_block0)
    def _():
        o_ref[...] = acc.astype(o_ref.dtype)

    @pl.when(j >= gate_block0)
    def _():
        o_ref[...] = jax.nn.sigmoid(acc + b_ref[...]).astype(o_ref.dtype)


def _proj_in(a, w, b_full, off_gate):
    m, k = a.shape
    n = w.shape[1]
    tm = _tile(m, 1024)
    tn = _tile(np.gcd(n, off_gate), 1024)
    return pl.pallas_call(
        functools.partial(_proj_in_kernel, gate_block0=off_gate // tn),
        grid=(m // tm, n // tn),
        in_specs=[pl.BlockSpec((tm, k), lambda i, j: (i, 0)),
                  pl.BlockSpec((k, tn), lambda i, j: (0, j)),
                  pl.BlockSpec((1, tn), lambda i, j: (0, j))],
        out_specs=pl.BlockSpec((tm, tn), lambda i, j: (i, j)),
        out_shape=jax.ShapeDtypeStruct((m, n), BF16),
        compiler_params=_params(2),
    )(a, w, b_full)


def _mm_kernel(a_ref, w_ref, o_ref):
    o_ref[...] = _dot(a_ref[...], w_ref[...]).astype(o_ref.dtype)


def _matmul_cols(a, w, col0, ncols):
    m, k = a.shape
    tm = _tile(m, 1024)
    tn = _tile(np.gcd(ncols, col0) if col0 else ncols, 512)
    j0 = col0 // tn
    return pl.pallas_call(
        _mm_kernel,
        grid=(m // tm, ncols // tn),
        in_specs=[pl.BlockSpec((tm, k), lambda i, j: (i, 0)),
                  pl.BlockSpec((k, tn), lambda i, j: (0, j0 + j))],
        out_specs=pl.BlockSpec((tm, tn), lambda i, j: (i, j)),
        out_shape=jax.ShapeDtypeStruct((m, ncols), BF16),
        compiler_params=_params(2),
    )(a, w)


def _dft_tables(n, scale):
    k = np.arange(n)
    ang = 2.0 * np.pi * ((k[:, None] * k[None, :]) % n) / n
    return np.cos(ang) * scale, np.sin(ang) * scale


def _fourier_chan_kernel(u_ref, t_ref, o_ref):
    dg = u_ref.shape[-1]
    res = _dot(u_ref[0], t_ref[...])
    o_ref[0, 0] = res[:, :dg].astype(o_ref.dtype)
    o_ref[0, 1] = res[:, dg:].astype(o_ref.dtype)


def _fourier_chan(z3, f_width):
    b, l, _ = z3.shape
    dg = f_width // F_GROUPS
    cd, sd = _dft_tables(dg, dg ** -0.5)
    table = jnp.asarray(np.concatenate([cd, sd], axis=1), dtype=BF16)
    tm = _tile(l, 1024)
    return pl.pallas_call(
        _fourier_chan_kernel,
        grid=(b, l // tm, F_GROUPS),
        in_specs=[pl.BlockSpec((1, tm, dg), lambda bi, i, g: (bi, i, g)),
                  pl.BlockSpec((dg, 2 * dg), lambda bi, i, g: (0, 0))],
        out_specs=pl.BlockSpec((1, 2, tm, dg), lambda bi, i, g: (bi, 0, i, g)),
        out_shape=jax.ShapeDtypeStruct((b, 2, l, f_width), BF16),
        compiler_params=_params(3),
    )(z3, table)


def _fourier_pos_kernel(t_ref, ab_ref, o_ref):
    o_ref[0] = _dot(t_ref[...], ab_ref[0]).astype(o_ref.dtype)


def _fourier_pos(ab):
    b, l2, fw = ab.shape
    l = l2 // 2
    cl, sl = _dft_tables(l, l ** -0.5)
    table = jnp.asarray(np.concatenate([cl, -sl], axis=1), dtype=BF16)
    tm = _tile(l, 1024)
    tn = _tile(fw, 512)
    return pl.pallas_call(
        _fourier_pos_kernel,
        grid=(b, fw // tn, l // tm),
        in_specs=[pl.BlockSpec((tm, l2), lambda bi, j, i: (i, 0)),
                  pl.BlockSpec((1, l2, tn), lambda bi, j, i: (bi, 0, j))],
        out_specs=pl.BlockSpec((1, tm, tn), lambda bi, j, i: (bi, i, j)),
        out_shape=jax.ShapeDtypeStruct((b, l, fw), BF16),
        compiler_params=_params(3),
    )(table, ab)


def _na_geometry(rows):
    kr = min(NA_ROWS_MAX, rows)
    starts = [int(np.clip(r - kr // 2, 0, rows - kr)) - r + NA_ROWS_MAX - 1 for r in range(rows)]
    return kr, min(starts), max(starts) - min(starts) + 1


def _na_bias_slabs(rel_bias, rows):
    kr, min_start, n_slabs = _na_geometry(rows)
    cols = np.arange(GRID_W)
    win_start = np.clip(cols - NA_COLS // 2, 0, GRID_W - NA_COLS)
    rel_col = cols[None, :] - win_start[:, None]
    col_mask = (rel_col >= 0) & (rel_col < NA_COLS)
    dc_idx = np.clip(cols[None, :] - cols[:, None] + NA_COLS - 1, 0, 2 * NA_COLS - 2)
    neg = jnp.finfo(F32).min
    bias_c = jnp.where(col_mask[None, None], rel_bias.astype(F32)[:, :, dc_idx], neg)
    slabs = jnp.stack([bias_c[:, min_start + s:min_start + s + kr] for s in range(n_slabs)], axis=1)
    slabs = jnp.transpose(slabs, (0, 1, 3, 2, 4))
    h = rel_bias.shape[0]
    return slabs.reshape(h, n_slabs, GRID_W, kr * GRID_W)


def _na_kernel(q_ref, k_ref, v_ref, kc_ref, vc_ref, bias_ref, o_ref, s_ctx_all, p_ctx_all, o_loc_all,
               *, heads_per_step, **geometry):
    for hh in range(heads_per_step):
        cols = slice(hh * NA_HEAD_DIM, (hh + 1) * NA_HEAD_DIM)
        _na_one_head(q_ref.at[0, :, cols], k_ref.at[0, :, cols], v_ref.at[0, :, cols], kc_ref.at[0, :, cols],
                     vc_ref.at[0, :, cols], bias_ref.at[hh], o_ref.at[0, :, cols],
                     s_ctx_all, p_ctx_all, o_loc_all, **geometry)


def _na_one_head(q_ref, k_ref, v_ref, kc_ref, vc_ref, bias_ref, o_ref, s_ctx_all, p_ctx_all, o_loc_all,
                 *, rows, rows_per_step, kr, min_start, scale):
    s_ctx_all[...] = lax.dot_general(q_ref[...], kc_ref[...], NT_DIMS, preferred_element_type=F32) * scale

    def scores(r):
        rs = jnp.clip(r - kr // 2, 0, rows - kr)
        slab = rs - r + (NA_ROWS_MAX - 1 - min_start)
        q0 = pl.multiple_of(r * GRID_W, GRID_W)
        k0 = pl.multiple_of(rs * GRID_W, GRID_W)
        qr = q_ref[pl.ds(q0, GRID_W), :]
        kw = k_ref[pl.ds(k0, kr * GRID_W), :]
        s_loc = lax.dot_general(qr, kw, NT_DIMS, preferred_element_type=F32) * scale + bias_ref[slab]
        return q0, k0, s_loc

    def softmax(q0, s_loc):
        s_ctx = s_ctx_all[pl.ds(q0, GRID_W), :]
        m = jnp.maximum(jnp.max(s_loc, axis=-1, keepdims=True), jnp.max(s_ctx, axis=-1, keepdims=True))
        p_loc = jnp.exp(s_loc - m)
        p_ctx = jnp.exp(s_ctx - m)
        inv = 1.0 / (jnp.sum(p_loc, axis=-1, keepdims=True) + jnp.sum(p_ctx, axis=-1, keepdims=True))
        return (p_loc * inv).astype(BF16), (p_ctx * inv).astype(BF16)

    def row_group(g, carry):
        sc = [scores(g * rows_per_step + u) for u in range(rows_per_step)]
        pr = [softmax(q0, s_loc) for q0, _, s_loc in sc]
        for (q0, k0, _), (p_loc, p_ctx) in zip(sc, pr):
            p_ctx_all[pl.ds(q0, GRID_W), :] = p_ctx
            o_loc_all[pl.ds(q0, GRID_W), :] = _dot(p_loc, v_ref[pl.ds(k0, kr * GRID_W), :])
        return carry

    lax.fori_loop(0, rows // rows_per_step, row_group, 0)
    o_ref[...] = (o_loc_all[...] + _dot(p_ctx_all[...], vc_ref[...])).astype(o_ref.dtype)


def _neighbourhood_attention(z3, zc3, bias_slabs, off_q, na_width):
    b, l, _ = z3.shape
    ctx_len = zc3.shape[1]
    dh = NA_HEAD_DIM
    heads = na_width // dh
    rows = l // GRID_W
    kr, min_start, n_slabs = _na_geometry(rows)
    hp = _tile(heads, NA_HEADS_PER_STEP)
    w = hp * dh
    qb, kb, vb = off_q // w, (off_q + na_width) // w, (off_q + 2 * na_width) // w
    kern = functools.partial(_na_kernel, heads_per_step=hp, rows=rows,
                             rows_per_step=_tile(rows, NA_ROWS_PER_STEP), kr=kr,
                             min_start=min_start, scale=dh ** -0.5)
    return pl.pallas_call(
        kern,
        grid=(heads // hp, b),
        in_specs=[pl.BlockSpec((1, l, w), lambda h, bi: (bi, 0, qb + h)),
                  pl.BlockSpec((1, l, w), lambda h, bi: (bi, 0, kb + h)),
                  pl.BlockSpec((1, l, w), lambda h, bi: (bi, 0, vb + h)),
                  pl.BlockSpec((1, ctx_len, w), lambda h, bi: (bi, 0, h)),
                  pl.BlockSpec((1, ctx_len, w), lambda h, bi: (bi, 0, heads // hp + h)),
                  pl.BlockSpec((hp, n_slabs, GRID_W, kr * GRID_W), lambda h, bi: (h, 0, 0, 0))],
        out_specs=pl.BlockSpec((1, l, w), lambda h, bi: (bi, 0, h)),
        out_shape=jax.ShapeDtypeStruct((b, l, na_width), BF16),
        scratch_shapes=[pltpu.VMEM((l, ctx_len), F32), pltpu.VMEM((l, ctx_len), BF16),
                        pltpu.VMEM((l, dh), F32)],
        compiler_params=_params(2),
    )(z3, z3, z3, zc3, zc3, bias_slabs)


def _merge_kernel(yf_ref, on_ref, wf_ref, wn_ref, gf_ref, gn_ref, o_ref, wf_s, wn_s):
    @pl.when(pl.program_id(1) == 0)
    def _():
        wf_s[...] = wf_ref[...].astype(BF16)
        wn_s[...] = wn_ref[...].astype(BF16)

    y_f = _dot(yf_ref[...], wf_s[...])
    y_n = _dot(on_ref[...], wn_s[...])
    o_ref[...] = (gf_ref[...].astype(F32) * y_f + gn_ref[...].astype(F32) * y_n).astype(o_ref.dtype)


def _merge(y_four, o_na, w_f, w_n, z, off_gate):
    m, kf = y_four.shape
    kn = o_na.shape[1]
    d = w_f.shape[1]
    tm = _tile(m, 1024)
    tn = _tile(np.gcd(d, off_gate), 512)
    gf0, gn0 = off_gate // tn, (off_gate + d) // tn
    return pl.pallas_call(
        _merge_kernel,
        grid=(d // tn, m // tm),
        in_specs=[pl.BlockSpec((tm, kf), lambda j, i: (i, 0)),
                  pl.BlockSpec((tm, kn), lambda j, i: (i, 0)),
                  pl.BlockSpec((kf, tn), lambda j, i: (0, j)),
                  pl.BlockSpec((kn, tn), lambda j, i: (0, j)),
                  pl.BlockSpec((tm, tn), lambda j, i: (i, gf0 + j)),
                  pl.BlockSpec((tm, tn), lambda j, i: (i, gn0 + j))],
        out_specs=pl.BlockSpec((tm, tn), lambda j, i: (i, j)),
        out_shape=jax.ShapeDtypeStruct((m, d), BF16),
        scratch_shapes=[pltpu.VMEM((kf, tn), BF16), pltpu.VMEM((kn, tn), BF16)],
        compiler_params=_params(2),
    )(y_four, o_na, w_f, w_n, z, z)


def _out_proj_kernel(a_ref, w_ref, x_ref, gt_ref, o_ref, w_s):
    @pl.when(pl.program_id(1) == 0)
    def _():
        w_s[...] = w_ref[...].astype(BF16)

    o_ref[...] = x_ref[...] + gt_ref[0] * _dot(a_ref[...], w_s[...])


def _out_proj_residual(a, w, x2d, mod, k_gate, seq):
    m, k = a.shape
    d = w.shape[1]
    tm = _tile(seq, 1024)
    tn = _tile(d, 512)
    gate0 = k_gate * (d // tn)
    return pl.pallas_call(
        _out_proj_kernel,
        grid=(d // tn, m // tm),
        in_specs=[pl.BlockSpec((tm, k), lambda j, i: (i, 0)),
                  pl.BlockSpec((k, tn), lambda j, i: (0, j)),
                  pl.BlockSpec((tm, tn), lambda j, i: (i, j)),
                  pl.BlockSpec((1, 1, tn), lambda j, i: (i * tm // seq, 0, gate0 + j))],
        out_specs=pl.BlockSpec((tm, tn), lambda j, i: (i, j)),
        out_shape=jax.ShapeDtypeStruct((m, d), F32),
        scratch_shapes=[pltpu.VMEM((k, tn), BF16)],
        compiler_params=_params(2),
    )(a, w, x2d, mod)


def _router_kernel(x_ref, g_ref, sc_ref, sh_ref, wr_ref, aff_ref, xp_ref):
    xn = _rms_modulate(x_ref[0], g_ref[...], sc_ref[0], sh_ref[0])
    half = xn.shape[1] // 2
    xp_ref[0] = _pack_bf16_pair(xn[:, :half], xn[:, half:])
    x_hi = xn.astype(BF16)
    x_lo = (xn - x_hi.astype(F32)).astype(BF16)
    w = wr_ref[...]
    w_hi = w.astype(BF16)
    w_lo = (w - w_hi.astype(F32)).astype(BF16)
    logits = (lax.dot_general(w_hi, x_hi, NT_DIMS, preferred_element_type=F32)
              + lax.dot_general(w_lo, x_hi, NT_DIMS, preferred_element_type=F32)
              + lax.dot_general(w_hi, x_lo, NT_DIMS, preferred_element_type=F32))
    ex = jnp.exp(logits - jnp.max(logits, axis=0, keepdims=True))
    aff_ref[0] = ex / jnp.sum(ex, axis=0, keepdims=True)


def _router(x, gain, mod, k_shift, k_scale, w_router_t):
    b, l, d = x.shape
    e = w_router_t.shape[0]
    tm = _tile(l, 256)
    return pl.pallas_call(
        _router_kernel,
        grid=(b, l // tm),
        in_specs=[pl.BlockSpec((1, tm, d), lambda bi, i: (bi, i, 0)),
                  pl.BlockSpec((1, d), lambda bi, i: (0, 0)),
                  pl.BlockSpec((1, 1, d), lambda bi, i: (bi, 0, k_scale)),
                  pl.BlockSpec((1, 1, d), lambda bi, i: (bi, 0, k_shift)),
                  pl.BlockSpec((e, d), lambda bi, i: (0, 0))],
        out_specs=[pl.BlockSpec((1, e, tm), lambda bi, i: (bi, 0, i)),
                   pl.BlockSpec((1, tm, d // 2), lambda bi, i: (bi, i, 0))],
        out_shape=[jax.ShapeDtypeStruct((b, e, l), F32), jax.ShapeDtypeStruct((b, l, d // 2), jnp.uint32)],
        compiler_params=_params(2),
    )(x, gain.reshape(1, d), mod, mod, w_router_t)


def _select_kernel(a_ref, tri_ref, same_ref, lower_ref, idx_ref, dst_ref, gate_ref, cum_ref, pos_ref, rank_ref,
                   *, cap):
    a = a_ref[...]
    n_rows, l = a.shape
    capf = float(cap)

    def count(mask):
        return jnp.sum(jnp.where(mask, 1.0, 0.0), axis=-1, keepdims=True)

    def halve(_, lo_hi):
        lo, hi = lo_hi
        mid = 0.5 * lo + 0.5 * hi
        ok = count(a_ref[...] >= mid) >= capf
        return jnp.where(ok, mid, lo), jnp.where(ok, hi, mid)

    _, hi = lax.fori_loop(0, BISECT_STEPS, halve,
                          (jnp.zeros((n_rows, 1), F32), jnp.full((n_rows, 1), 2.0, F32)))
    tau = jnp.max(jnp.where(a < hi, a, -1.0), axis=-1, keepdims=True)
    above = a > tau
    tied = a == tau
    need = capf - count(above)
    tri = tri_ref[...]
    tied_before = _dot(jnp.where(tied, 1.0, 0.0).astype(BF16), tri)
    chosen = above | (tied & (tied_before < need))
    chosen_b = jnp.where(chosen, 1.0, 0.0).astype(BF16)
    pos = _dot(chosen_b, tri)
    pos_ref[...] = jnp.where(chosen, pos, -1.0)
    per_token = _dot(same_ref[...], chosen_b)
    before_token = _dot(per_token.astype(BF16), tri)
    cum_ref[...] = before_token
    rank_ref[...] = before_token + _dot(lower_ref[...], chosen_b)

    split = float(SPLIT_BASE)
    slot = lax.broadcasted_iota(jnp.int32, (cap, l), 0).astype(F32)
    piece = lax.broadcasted_iota(jnp.int32, (N_PIECES, l), 0)
    tok = lax.broadcasted_iota(jnp.int32, (1, l), 1).astype(F32)
    tok_hi = jnp.floor(tok * (1.0 / split))
    tok_lo = tok - split * tok_hi

    def compact(r, carry):
        hit = jnp.where(pos_ref[pl.ds(r, 1), :] == slot, 1.0, 0.0).astype(BF16)
        rank = rank_ref[pl.ds(r, 1), :]
        rank_hi = jnp.floor(rank * (1.0 / split))
        aff = a_ref[pl.ds(r, 1), :]
        aff_1 = aff.astype(BF16).astype(F32)
        aff_2 = (aff - aff_1).astype(BF16).astype(F32)
        rows = (tok_hi, tok_lo, rank_hi, rank - split * rank_hi, aff_1, aff_2, aff - aff_1 - aff_2)
        vals = jnp.zeros((N_PIECES, l), F32)
        for k, v in enumerate(rows):
            vals = jnp.where(piece == k, v, vals)
        got = lax.dot_general(vals.astype(BF16), hit, NT_DIMS, preferred_element_type=F32)
        idx_ref[pl.ds(r, 1), :] = (split * got[0:1] + got[1:2]).astype(jnp.int32)
        dst_ref[pl.ds(r, 1), :] = (split * got[2:3] + got[3:4]).astype(jnp.int32)
        gate_ref[pl.ds(r, 1), :] = got[4:5] + got[5:6] + got[6:7]
        return carry

    lax.fori_loop(0, n_rows, compact, 0)


def _select(aff_rows, cap, n_experts):
    n_rows, l = aff_rows.shape
    tri = jnp.asarray(np.triu(np.ones((l, l), np.float32), k=1), dtype=BF16)
    group = np.arange(n_rows) // n_experts
    same = group[:, None] == group[None, :]
    lower = same & (np.arange(n_rows)[None, :] < np.arange(n_rows)[:, None])
    assert l <= SPLIT_BASE ** 2 and n_experts * cap <= SPLIT_BASE ** 2
    full = lambda shape: pl.BlockSpec(shape, lambda i: (0, 0))
    per_slot = jax.ShapeDtypeStruct((n_rows, cap), jnp.int32)
    return pl.pallas_call(
        functools.partial(_select_kernel, cap=cap),
        grid=(1,),
        in_specs=[full((n_rows, l)), full((l, l)), full((n_rows, n_rows)), full((n_rows, n_rows))],
        out_specs=[full((n_rows, cap)), full((n_rows, cap)), full((n_rows, cap)), full((n_rows, l))],
        out_shape=[per_slot, per_slot, jax.ShapeDtypeStruct((n_rows, cap), F32),
                   jax.ShapeDtypeStruct((n_rows, l), F32)],
        scratch_shapes=[pltpu.VMEM((n_rows, l), F32), pltpu.VMEM((n_rows, l), F32)],
        compiler_params=_params(1),
    )(aff_rows, tri, jnp.asarray(same, dtype=BF16), jnp.asarray(lower, dtype=BF16))


def _gather_rows_kernel(rows_ref, src_hbm, o_ref, sem):
    n_rows = o_ref.shape[0]
    base = pl.program_id(0) * n_rows

    def start_group(g, carry):
        for u in range(DMA_ISSUE_UNROLL):
            j = g * DMA_ISSUE_UNROLL + u
            pltpu.make_async_copy(src_hbm.at[pl.ds(rows_ref[base + j], 1)], o_ref.at[pl.ds(j, 1)],
                                  sem).start(priority=u % 2)
        return carry

    lax.fori_loop(0, n_rows // DMA_ISSUE_UNROLL, start_group, 0)
    pltpu.make_async_copy(src_hbm.at[pl.ds(0, n_rows)], o_ref, sem).wait()


def _gather_rows(flat_rows, src):
    n = flat_rows.shape[0]
    w = src.shape[1]
    r = _tile(n, 1024)
    grid_spec = pltpu.PrefetchScalarGridSpec(
        num_scalar_prefetch=1,
        grid=(n // r,),
        in_specs=[pl.BlockSpec(memory_space=pl.ANY)],
        out_specs=pl.BlockSpec((r, w), lambda i, rows: (i, 0)),
        scratch_shapes=[pltpu.SemaphoreType.DMA],
    )
    return pl.pallas_call(
        _gather_rows_kernel,
        grid_spec=grid_spec,
        out_shape=jax.ShapeDtypeStruct((n, w), src.dtype),
        compiler_params=_params(1),
    )(flat_rows, src)


def _ffn_up_kernel(xp_ref, w1_ref, w3_ref, o_ref, x_s):
    half = xp_ref.shape[1]

    @pl.when(pl.program_id(2) == 0)
    def _():
        lo, hi = _unpack_bf16_pair(xp_ref[...])
        x_s[:, :half] = lo
        x_s[:, half:] = hi

    xin = x_s[...]
    a = _dot(xin, w1_ref[0].astype(BF16))
    o_ref[...] = (a * jax.nn.sigmoid(a) * _dot(xin, w3_ref[0].astype(BF16))).astype(o_ref.dtype)


def _ffn_up(xin_packed, w1, w3):
    n, half = xin_packed.shape
    e, d, ff = w1.shape
    rows_e = n // e
    tm = _tile(rows_e, 1024)
    tn = _tile(ff, 256)
    per_e = rows_e // tm
    return pl.pallas_call(
        _ffn_up_kernel,
        grid=(e, per_e, ff // tn),
        in_specs=[pl.BlockSpec((tm, half), lambda ei, i, j: (ei * per_e + i, 0)),
                  pl.BlockSpec((1, d, tn), lambda ei, i, j: (ei, 0, j)),
                  pl.BlockSpec((1, d, tn), lambda ei, i, j: (ei, 0, j))],
        out_specs=pl.BlockSpec((tm, tn), lambda ei, i, j: (ei * per_e + i, j)),
        out_shape=jax.ShapeDtypeStruct((n, ff), BF16),
        scratch_shapes=[pltpu.VMEM((tm, d), BF16)],
        compiler_params=_params(3),
    )(xin_packed, w1, w3)


def _bf16_bits(x):
    return lax.bitcast_convert_type(x.astype(BF16).astype(F32), jnp.uint32)


def _pack_bf16_pair(lo, hi):
    return (_bf16_bits(lo) >> 16) | _bf16_bits(hi)


def _unpack_bf16_pair(words):
    lo = lax.bitcast_convert_type(words << 16, F32).astype(BF16)
    hi = lax.bitcast_convert_type(words & jnp.uint32(0xFFFF0000), F32).astype(BF16)
    return lo, hi


def _ffn_down_kernel(h_ref, wa_ref, wb_ref, g_ref, o_ref):
    h = h_ref[...]
    gate = g_ref[...]
    o_ref[...] = _pack_bf16_pair(_dot(h, wa_ref[0].astype(BF16)) * gate,
                                 _dot(h, wb_ref[0].astype(BF16)) * gate)


def _ffn_down(hid, w2, gate_col):
    n, ff = hid.shape
    e, _, d = w2.shape
    rows_e = n // e
    half = d // 2
    tw = _tile(half, 256)
    nb = half // tw
    return pl.pallas_call(
        _ffn_down_kernel,
        grid=(e, nb),
        in_specs=[pl.BlockSpec((rows_e, ff), lambda ei, j: (ei, 0)),
                  pl.BlockSpec((1, ff, tw), lambda ei, j: (ei, 0, j)),
                  pl.BlockSpec((1, ff, tw), lambda ei, j: (ei, 0, nb + j)),
                  pl.BlockSpec((rows_e, 1), lambda ei, j: (ei, 0))],
        out_specs=pl.BlockSpec((rows_e, tw), lambda ei, j: (ei, j)),
        out_shape=jax.ShapeDtypeStruct((n, half), jnp.uint32),
        compiler_params=_params(2),
    )(hid, w2, w2, gate_col)


def _row_copy(src, dst_hbm, sem, j, row):
    return pltpu.make_async_copy(src.at[pl.ds(j, 1)], dst_hbm.at[pl.ds(row, 1)], sem)


def _permute_kernel(dest_ref, src_ref, dst_hbm, sem):
    n_rows = src_ref.shape[0]
    base = pl.program_id(0) * n_rows

    def start_group(g, carry):
        for u in range(DMA_ISSUE_UNROLL):
            j = g * DMA_ISSUE_UNROLL + u
            _row_copy(src_ref, dst_hbm, sem, j, dest_ref[base + j]).start(priority=u % 2)
        return carry

    lax.fori_loop(0, n_rows // DMA_ISSUE_UNROLL, start_group, 0)
    pltpu.make_async_copy(src_ref, dst_hbm.at[pl.ds(0, n_rows)], sem).wait()


def _permute_rows(rows, dest):
    n, w = rows.shape
    r = _tile(n, 1024)
    grid_spec = pltpu.PrefetchScalarGridSpec(
        num_scalar_prefetch=1,
        grid=(n // r,),
        in_specs=[pl.BlockSpec((r, w), lambda i, dest_ref: (i, 0))],
        out_specs=pl.BlockSpec(memory_space=pl.ANY),
        scratch_shapes=[pltpu.SemaphoreType.DMA],
    )
    return pl.pallas_call(
        _permute_kernel,
        grid_spec=grid_spec,
        out_shape=jax.ShapeDtypeStruct((n, w), rows.dtype),
        compiler_params=_params(1),
    )(dest, rows)


def _combine_kernel(start_ref, nchunks_ref, first_ref, rows_hbm, lo_ref, hi_ref, x_ref, gt_ref, fg_ref,
                    o_ref, buf, sem, acc_lo, acc_hi, *, n_total):
    step = pl.program_id(0) * pl.num_programs(1) + pl.program_id(1)
    n_steps = pl.num_programs(0) * pl.num_programs(1)
    chunk = buf.shape[1]
    tm = acc_lo.shape[0]
    n_chunks = nchunks_ref[step]
    first = first_ref[step]

    def row0(s, c):
        return pl.multiple_of(jnp.minimum(start_ref[s] + c * chunk, n_total - chunk), SUBLANES)

    def fetch(s, c, slot):
        return pltpu.make_async_copy(rows_hbm.at[pl.ds(row0(s, c), chunk)], buf.at[slot], sem.at[slot])

    @pl.when(step == 0)
    def _():
        fetch(0, 0, 0).start()

    acc_lo[...] = jnp.zeros_like(acc_lo)
    acc_hi[...] = jnp.zeros_like(acc_hi)
    row_lo = lo_ref[0]
    row_hi = hi_ref[0]

    def consume(c, carry):
        slot = (first + c) % 2
        ends_step = c + 1 == n_chunks

        @pl.when(jnp.logical_not(ends_step))
        def _():
            fetch(step, c + 1, 1 - slot).start()

        @pl.when(jnp.logical_and(ends_step, step + 1 < n_steps))
        def _():
            fetch(step + 1, 0, 1 - slot).start()

        fetch(step, c, slot).wait()
        lo, hi = _unpack_bf16_pair(buf[slot])
        row = (row0(step, c) + lax.broadcasted_iota(jnp.int32, (tm, chunk), 1)).astype(F32)
        first_row = jnp.maximum(row_lo, (start_ref[step] + c * chunk).astype(F32))
        onehot = jnp.where(jnp.logical_and(row >= first_row, row < row_hi), 1.0, 0.0).astype(BF16)
        acc_lo[...] += _dot(onehot, lo)
        acc_hi[...] += _dot(onehot, hi)
        return carry

    lax.fori_loop(0, n_chunks, consume, 0)
    y = jnp.concatenate([acc_lo[...], acc_hi[...]], axis=-1)
    x2 = x_ref[0] + gt_ref[0] * y
    ms = jnp.mean(x2 * x2, axis=-1, keepdims=True)
    o_ref[0] = x2 * lax.rsqrt(ms + RMS_EPS) * fg_ref[...]


def _combine(sorted_rows, row_lo, row_hi, x, mod, k_gate, final_g):
    b, l, d = x.shape
    n_total, half = sorted_rows.shape
    tm = _tile(l, 256)
    nt = l // tm
    chunk = COMBINE_CHUNK
    assert n_total >= chunk and n_total % SUBLANES == 0
    tile_lo = row_lo[:, ::tm].astype(jnp.int32).reshape(-1)
    tile_hi = row_hi[:, tm - 1::tm].astype(jnp.int32).reshape(-1)
    start = tile_lo // SUBLANES * SUBLANES
    n_chunks = jnp.maximum(1, (tile_hi - start + chunk - 1) // chunk)
    first = jnp.cumsum(n_chunks) - n_chunks
    col = lambda a: a.reshape(b, l, 1)
    grid_spec = pltpu.PrefetchScalarGridSpec(
        num_scalar_prefetch=3,
        grid=(b, nt),
        in_specs=[pl.BlockSpec(memory_space=pl.ANY),
                  pl.BlockSpec((1, tm, 1), lambda bi, i, *_: (bi, i, 0)),
                  pl.BlockSpec((1, tm, 1), lambda bi, i, *_: (bi, i, 0)),
                  pl.BlockSpec((1, tm, d), lambda bi, i, *_: (bi, i, 0)),
                  pl.BlockSpec((1, 1, d), lambda bi, i, *_: (bi, 0, k_gate)),
                  pl.BlockSpec((1, d), lambda bi, i, *_: (0, 0))],
        out_specs=pl.BlockSpec((1, tm, d), lambda bi, i, *_: (bi, i, 0)),
        scratch_shapes=[pltpu.VMEM((2, chunk, half), jnp.uint32), pltpu.SemaphoreType.DMA((2,)),
                        pltpu.VMEM((tm, half), F32), pltpu.VMEM((tm, half), F32)],
    )
    return pl.pallas_call(
        functools.partial(_combine_kernel, n_total=n_total),
        grid_spec=grid_spec,
        out_shape=jax.ShapeDtypeStruct((b, l, d), F32),
        compiler_params=_params(2),
    )(start, n_chunks.astype(jnp.int32), first.astype(jnp.int32), sorted_rows, col(row_lo), col(row_hi),
      x, mod, final_g.reshape(1, d))


def kernel(x, c, ctx, c_ctx, w_mod, b_mod, norm_mix_g, w_in, b_gate, w_fourier, na_rel_bias,
           w_na_out, w_out, norm_ffn_g, w_router, w1, w3, w2, final_norm_g):
    assert w_mod.shape[0] == 1, "single-layer stack only"
    b, l, d = x.shape
    ctx_len = ctx.shape[1]
    f_width = w_fourier.shape[1]
    na_width = w_na_out.shape[1]
    n_experts = w_router.shape[-1]
    in_width = w_in.shape[-1]
    off_q = f_width
    off_k = off_q + na_width
    off_g = off_k + 2 * na_width
    cap = EC_FACTOR * l // n_experts
    SH_M, SC_M, GT_M, SH_F, SC_F, GT_F = range(N_MOD)

    pad = (-(b + 1)) % 16
    cond = jnp.concatenate([c, c_ctx[None], jnp.zeros((pad, d), F32)], axis=0)
    mod_all = _ada_params(cond, w_mod[0], b_mod[0])
    mod = mod_all[:b].reshape(b, 1, N_MOD * d)
    mod_ctx = mod_all[b:b + 1].reshape(1, 1, N_MOD * d)

    w_in_bf = w_in[0].astype(BF16)
    xn = _norm_mod(x, norm_mix_g[0], mod, SH_M, SC_M)
    cn = _norm_mod(ctx, norm_mix_g[0], mod_ctx, SH_M, SC_M)
    b_full = jnp.concatenate([jnp.zeros((off_g,), F32), b_gate[0]]).reshape(1, in_width)
    z = _proj_in(xn.reshape(b * l, d), w_in_bf, b_full, off_g)
    zc = _matmul_cols(cn.reshape(b * ctx_len, d), w_in_bf, off_k, 2 * na_width)
    z3 = z.reshape(b, l, in_width)

    ab = _fourier_chan(z3, f_width)
    y_four = _fourier_pos(ab.reshape(b, 2 * l, f_width))
    bias_slabs = _na_bias_slabs(na_rel_bias[0], l // GRID_W)
    o_na = _neighbourhood_attention(z3, zc.reshape(b, ctx_len, 2 * na_width), bias_slabs, off_q, na_width)
    mixed = _merge(y_four.reshape(b * l, f_width), o_na.reshape(b * l, na_width),
                   w_fourier[0], w_na_out[0], z, off_g)
    x1 = _out_proj_residual(mixed, w_out[0], x.reshape(b * l, d), mod, GT_M, l)

    x1_3 = x1.reshape(b, l, d)
    aff_t, xn_packed = _router(x1_3, norm_ffn_g[0], mod, SH_F, SC_F, w_router[0].T)
    idx_t, dst_t, gate_t, cum = _select(aff_t.reshape(b * n_experts, l), cap, n_experts)

    def expert_major(per_slot, batch_stride):
        v = per_slot.reshape(b, n_experts, cap)
        offs = (jnp.arange(b, dtype=jnp.int32) * batch_stride)[None, :, None]
        return (jnp.transpose(v, (1, 0, 2)) + offs.astype(v.dtype)).reshape(-1)

    per_batch = n_experts * cap
    xin = _gather_rows(expert_major(idx_t, l), xn_packed.reshape(b * l, d // 2))
    hid = _ffn_up(xin, w1[0], w3[0])
    out = _ffn_down(hid, w2[0], expert_major(gate_t, 0).reshape(-1, 1))
    sorted_rows = _permute_rows(out, expert_major(dst_t, per_batch))
    before = cum.reshape(b, n_experts, l)[:, 0, :] + (jnp.arange(b, dtype=F32) * per_batch)[:, None]
    after = jnp.concatenate([before[:, 1:], before[:, :1] + per_batch], axis=1)
    return _combine(sorted_rows, before, after, x1_3, mod, GT_F, final_norm_g)
```
